```python
import math, functools
import jax, jax.numpy as jnp
from jax import lax
import numpy as np

D_MODEL = 1024
BATCH = 8
SEQ = 4096
DEPTH = 1
DEC_BATCH = 32
DEC_SEQ = 1
PAST_LEN = 16384
PAGE_SIZE = 128

DA_HEADS = D_MODEL // 256
DA_QK = 64
DA_V = 2 * DA_QK
DA_QCOLS = DA_HEADS * 2 * DA_QK
DA_COLS = 2 * DA_QCOLS + DA_HEADS * DA_V
DA_SCALE = DA_QK ** -0.5
RW_DIM = D_MODEL // 2
RW_HEAD = 64
RW_HEADS = RW_DIM // RW_HEAD
RW_W_LORA = 64
RW_A_LORA = 64
RW_G_LORA = 128
RW_PROJ = 3 * RW_DIM + RW_W_LORA + RW_A_LORA + RW_G_LORA
RW_GN_EPS = 64e-5
IN_COLS = DA_COLS + RW_PROJ
MIX_WIDTH = DA_HEADS * DA_V + RW_DIM
D_FF = ((8 * D_MODEL // 3 + 127) // 128) * 128
ROPE_THETA = 10000.0
QBLOCK = 128
ALPHA = (2 * DEPTH) ** 0.25
BETA = (8 * DEPTH) ** -0.25
LN_EPS = 1e-5
NEG = -1e30

kernel_name = 'hymba_diffattn_rwkv7_macaron_deepnorm_step'


def _layernorm(x, g, b):
    xf = x.astype(jnp.float32)
    mu = jnp.mean(xf, -1, keepdims=True)
    var = jnp.mean(jnp.square(xf - mu), -1, keepdims=True)
    return ((xf - mu) * lax.rsqrt(var + LN_EPS) * g + b).astype(x.dtype)


def _swiglu(x, w_gu, w_down):
    gate, up = jnp.split(x @ w_gu, 2, axis=-1)
    return (jax.nn.silu(gate) * up) @ w_down


def _rope(x, pos):
    half = DA_QK // 2
    inv = ROPE_THETA ** (-jnp.arange(half, dtype=jnp.float32) / half)
    ang = pos.astype(jnp.float32)[:, None] * inv[None, :]
    cos = jnp.cos(ang)[None, :, None, None, :]
    sin = jnp.sin(ang)[None, :, None, None, :]
    xf = x.astype(jnp.float32)
    x1, x2 = xf[..., :half], xf[..., half:]
    return jnp.concatenate([x1 * cos - x2 * sin, x2 * cos + x1 * sin], -1).astype(x.dtype)


def _diff_weights(logits, lam):
    p = jax.nn.softmax(logits, axis=-1)
    return p[:, :, 0] - lam * p[:, :, 1]


def _attn_prompt(q, k, v, lam):
    B, S = q.shape[0], q.shape[1]
    nb = S // QBLOCK
    qb = jnp.swapaxes(q.reshape(B, nb, QBLOCK, DA_HEADS, 2, DA_QK), 0, 1)
    kpos = jnp.arange(S)

    def one(args):
        q_blk, i = args
        qpos = i * QBLOCK + jnp.arange(QBLOCK)
        logits = jnp.einsum('bqhmd,bkhmd->bhmqk', q_blk, k,
                            preferred_element_type=jnp.float32) * DA_SCALE
        logits = jnp.where(kpos[None, :] <= qpos[:, None], logits, NEG)
        attn = _diff_weights(logits, lam).astype(v.dtype)
        return jnp.einsum('bhqk,bkhe->bqhe', attn, v)

    out = lax.map(one, (qb, jnp.arange(nb)))
    return jnp.swapaxes(out, 0, 1).reshape(B, S, DA_HEADS, DA_V)


def _attn_sample(q, k, v, lam, k_past, v_past):
    T = q.shape[1]
    P = k_past.shape[1]
    lp = jnp.einsum('bqhmd,bkhmd->bhmqk', q, k_past, preferred_element_type=jnp.float32)
    ln = jnp.einsum('bqhmd,bkhmd->bhmqk', q, k, preferred_element_type=jnp.float32)
    ln = jnp.where(jnp.tril(jnp.ones((T, T), dtype=bool)), ln, NEG)
    logits = jnp.concatenate([lp, ln], axis=-1) * DA_SCALE
    attn = _diff_weights(logits, lam).astype(v.dtype)
    return (jnp.einsum('bhqk,bkhe->bqhe', attn[..., :P], v_past)
            + jnp.einsum('bhqk,bkhe->bqhe', attn[..., P:], v))


def _rwkv7(prw, shift, wkv, p):
    B, T, _ = prw.shape
    dt = prw.dtype
    prev = jnp.concatenate([shift.astype(dt), prw[:, :-1]], axis=1)
    xm = (prw + (prev - prw) * p['rw_mu']).astype(jnp.float32)
    o1, o2, o3 = RW_DIM, 2 * RW_DIM, 3 * RW_DIM
    o4 = o3 + RW_W_LORA
    o5 = o4 + RW_A_LORA
    r, k, v = xm[..., :o1], xm[..., o1:o2], xm[..., o2:o3]
    xw, xa, xg = xm[..., o3:o4], xm[..., o4:o5], xm[..., o5:]
    w_log = -jax.nn.softplus(-(p['rw_w0'] + jnp.tanh(xw) @ p['rw_w2'])) - 0.5
    decay = jnp.exp(-jnp.exp(w_log))
    a = jax.nn.sigmoid(p['rw_a0'] + xa @ p['rw_a2'])
    g = jax.nn.sigmoid(xg) @ p['rw_g2']
    hs = (B, T, RW_HEADS, RW_HEAD)
    kk = (k * p['rw_k_k']).reshape(hs)
    kk = kk / jnp.maximum(jnp.sqrt(jnp.sum(kk * kk, -1, keepdims=True)), 1e-12)
    k = k * (1.0 + (a - 1.0) * p['rw_k_a'])
    r, k, v, decay, a = [t.reshape(hs) for t in (r, k, v, decay, a)]
    b = kk * a
    xs = tuple(jnp.swapaxes(t, 0, 1) for t in (r, decay, k, v, kk, b))

    def step(S, inp):
        r_t, w_t, k_t, v_t, kk_t, b_t = inp
        sa = jnp.einsum('bhvk,bhk->bhv', S, -kk_t)
        S = (S * w_t[:, :, None, :] + sa[..., None] * b_t[:, :, None, :]
             + v_t[..., None] * k_t[:, :, None, :])
        return S, jnp.einsum('bhvk,bhk->bhv', S, r_t)

    S_fin, o = lax.scan(step, wkv.astype(jnp.float32), xs)
    o = jnp.swapaxes(o, 0, 1)
    mu = jnp.mean(o, -1, keepdims=True)
    var = jnp.mean(jnp.square(o - mu), -1, keepdims=True)
    o = ((o - mu) * lax.rsqrt(var + RW_GN_EPS) * p['rw_lnx_g'].reshape(RW_HEADS, RW_HEAD)
         + p['rw_lnx_b'].reshape(RW_HEADS, RW_HEAD))
    o = o + jnp.sum(r * k * p['rw_r_k'], -1, keepdims=True) * v
    o = o.reshape(B, T, RW_DIM) * g
    return o.astype(dt), S_fin.astype(wkv.dtype), prw[:, -1:]


def _layer(x, p, lam_init, positions, attn_fn, shift, wkv):
    B, T, _ = x.shape
    h = _layernorm(ALPHA * x + 0.5 * _swiglu(x, p['ffa_w_gu'], p['ffa_w_down']),
                   p['ln1_g'], p['ln1_b'])
    proj = h @ p['w_in']
    q = _rope(proj[..., :DA_QCOLS].reshape(B, T, DA_HEADS, 2, DA_QK), positions)
    k = _rope(proj[..., DA_QCOLS:2 * DA_QCOLS].reshape(B, T, DA_HEADS, 2, DA_QK), positions)
    v = proj[..., 2 * DA_QCOLS:DA_COLS].reshape(B, T, DA_HEADS, DA_V)
    prw = proj[..., DA_COLS:]
    lq1, lk1 = p['da_lq1'].astype(jnp.float32), p['da_lk1'].astype(jnp.float32)
    lq2, lk2 = p['da_lq2'].astype(jnp.float32), p['da_lk2'].astype(jnp.float32)
    lam = jnp.exp(jnp.sum(lq1 * lk1)) - jnp.exp(jnp.sum(lq2 * lk2)) + lam_init
    o_da = attn_fn(q, k, v, lam).astype(jnp.float32)
    o_da = (o_da * lax.rsqrt(jnp.mean(o_da * o_da, -1, keepdims=True) + LN_EPS)
            * p['da_subln_g'] * (1.0 - lam_init))
    o_rw, wkv_new, shift_new = _rwkv7(prw, shift, wkv, p)
    mix = jnp.concatenate([o_da.reshape(B, T, DA_HEADS * DA_V).astype(x.dtype), o_rw],
                          axis=-1) @ p['w_out']
    h = _layernorm(ALPHA * h + mix, p['ln2_g'], p['ln2_b'])
    y = _layernorm(ALPHA * h + 0.5 * _swiglu(h, p['ffb_w_gu'], p['ffb_w_down']),
                   p['ln3_g'], p['ln3_b'])
    return y, k.reshape(B, T, DA_HEADS, 2 * DA_QK), v, wkv_new, shift_new


def setup_inputs(seed: int = 0) -> dict:
    key = jax.random.key(seed)
    ks = iter(jax.random.split(key, 64))

    def nrm(shape, scale):
        return jax.random.normal(next(ks), shape, jnp.float32) * scale

    n_pages = PAST_LEN // PAGE_SIZE
    n_used = DEC_BATCH * n_pages
    n_pool = n_used + max(1, n_used // 4)
    page_table = jax.random.permutation(next(ks), n_pool)[:n_used].reshape(
        DEC_BATCH, n_pages).astype(jnp.int32)
    col_scale = jnp.concatenate([
        jnp.ones((2 * DA_QCOLS,), jnp.float32),
        jnp.full((DA_HEADS * DA_V,), BETA, jnp.float32),
        jnp.ones((2 * RW_DIM,), jnp.float32),
        jnp.full((RW_DIM,), BETA, jnp.float32),
        jnp.ones((RW_W_LORA + RW_A_LORA + RW_G_LORA,), jnp.float32)])
    L = DEPTH
    return {
        'x_prompt': nrm((BATCH, SEQ, D_MODEL), 1.0),
        'x_sample': nrm((DEC_BATCH, DEC_SEQ, D_MODEL), 1.0),
        'cache_k': nrm((L, n_pool, PAGE_SIZE, DA_HEADS, 2 * DA_QK), 1.0),
        'cache_v': nrm((L, n_pool, PAGE_SIZE, DA_HEADS, DA_V), BETA),
        'state_wkv': nrm((L, DEC_BATCH, RW_HEADS, RW_HEAD, RW_HEAD), 0.5),
        'state_shift': nrm((L, DEC_BATCH, 1, RW_PROJ), 1.0),
        'page_table': page_table,
        'ffa_w_gu': nrm((L, D_MODEL, 2 * D_FF), D_MODEL ** -0.5),
        'ffa_w_down': nrm((L, D_FF, D_MODEL), BETA * D_FF ** -0.5),
        'ffb_w_gu': nrm((L, D_MODEL, 2 * D_FF), D_MODEL ** -0.5),
        'ffb_w_down': nrm((L, D_FF, D_MODEL), BETA * D_FF ** -0.5),
        'ln1_g': 1.0 + nrm((L, D_MODEL), 0.05),
        'ln1_b': nrm((L, D_MODEL), 0.01),
        'ln2_g': 1.0 + nrm((L, D_MODEL), 0.05),
        'ln2_b': nrm((L, D_MODEL), 0.01),
        'ln3_g': 1.0 + nrm((L, D_MODEL), 0.05),
        'ln3_b': nrm((L, D_MODEL), 0.01),
        'w_in': nrm((L, D_MODEL, IN_COLS), D_MODEL ** -0.5) * col_scale,
        'w_out': nrm((L, MIX_WIDTH, D_MODEL), BETA * MIX_WIDTH ** -0.5),
        'da_lq1': nrm((L, DA_QK), 0.1),
        'da_lk1': nrm((L, DA_QK), 0.1),
        'da_lq2': nrm((L, DA_QK), 0.1),
        'da_lk2': nrm((L, DA_QK), 0.1),
        'da_subln_g': 1.0 + nrm((L, DA_V), 0.05),
        'rw_mu': jax.random.uniform(next(ks), (L, RW_PROJ), jnp.float32),
        'rw_w0': jax.random.uniform(next(ks), (L, RW_DIM), jnp.float32, -5.0, 0.5),
        'rw_w2': nrm((L, RW_W_LORA, RW_DIM), 0.5 * RW_W_LORA ** -0.5),
        'rw_a0': nrm((L, RW_DIM), 0.1),
        'rw_a2': nrm((L, RW_A_LORA, RW_DIM), RW_A_LORA ** -0.5),
        'rw_g2': nrm((L, RW_G_LORA, RW_DIM), RW_G_LORA ** -0.5),
        'rw_k_k': 0.85 + nrm((L, RW_DIM), 0.05),
        'rw_k_a': 1.0 + nrm((L, RW_DIM), 0.05),
        'rw_r_k': nrm((L, RW_HEADS, RW_HEAD), 0.1),
        'rw_lnx_g': 1.0 + nrm((L, RW_DIM), 0.05),
        'rw_lnx_b': nrm((L, RW_DIM), 0.01),
    }


def reference(x_prompt, x_sample, cache_k, cache_v, state_wkv, state_shift, page_table,
              ffa_w_gu, ffa_w_down, ffb_w_gu, ffb_w_down,
              ln1_g, ln1_b, ln2_g, ln2_b, ln3_g, ln3_b,
              w_in, w_out, da_lq1, da_lk1, da_lq2, da_lk2, da_subln_g,
              rw_mu, rw_w0, rw_w2, rw_a0, rw_a2, rw_g2, rw_k_k, rw_k_a, rw_r_k,
              rw_lnx_g, rw_lnx_b):
    B, S, _ = x_prompt.shape
    DB, T, _ = x_sample.shape
    n_pages = page_table.shape[1]
    past = n_pages * PAGE_SIZE
    pos_prompt = jnp.arange(S)
    pos_sample = past + jnp.arange(T)
    y_prompt, y_sample = x_prompt, x_sample
    kp_l, vp_l, wp_l, sp_l, ks_l, vs_l, ws_l, ss_l = [], [], [], [], [], [], [], []
    for l in range(DEPTH):
        p = dict(ffa_w_gu=ffa_w_gu[l], ffa_w_down=ffa_w_down[l],
                 ffb_w_gu=ffb_w_gu[l], ffb_w_down=ffb_w_down[l],
                 ln1_g=ln1_g[l], ln1_b=ln1_b[l], ln2_g=ln2_g[l], ln2_b=ln2_b[l],
                 ln3_g=ln3_g[l], ln3_b=ln3_b[l], w_in=w_in[l], w_out=w_out[l],
                 da_lq1=da_lq1[l], da_lk1=da_lk1[l], da_lq2=da_lq2[l], da_lk2=da_lk2[l],
                 da_subln_g=da_subln_g[l], rw_mu=rw_mu[l], rw_w0=rw_w0[l],
                 rw_w2=rw_w2[l], rw_a0=rw_a0[l], rw_a2=rw_a2[l], rw_g2=rw_g2[l],
                 rw_k_k=rw_k_k[l], rw_k_a=rw_k_a[l], rw_r_k=rw_r_k[l],
                 rw_lnx_g=rw_lnx_g[l], rw_lnx_b=rw_lnx_b[l])
        lam_init = 0.8 - 0.6 * math.exp(-0.3 * l)
        shift0 = jnp.zeros((B, 1, RW_PROJ), x_prompt.dtype)
        wkv0 = jnp.zeros((B, RW_HEADS, RW_HEAD, RW_HEAD), x_prompt.dtype)
        y_prompt, kp, vp, wp, sp = _layer(y_prompt, p, lam_init, pos_prompt,
                                          _attn_prompt, shift0, wkv0)
        k_past = cache_k[l][page_table].reshape(DB, past, DA_HEADS, 2, DA_QK)
        v_past = cache_v[l][page_table].reshape(DB, past, DA_HEADS, DA_V)
        attn_fn = functools.partial(_attn_sample, k_past=k_past, v_past=v_past)
        y_sample, ks, vs, ws, ss = _layer(y_sample, p, lam_init, pos_sample, attn_fn,
                                          state_shift[l], state_wkv[l])
        kp_l.append(kp); vp_l.append(vp); wp_l.append(wp); sp_l.append(sp)
        ks_l.append(ks); vs_l.append(vs); ws_l.append(ws); ss_l.append(ss)
    k_prompt = jnp.stack(kp_l)
    v_prompt = jnp.stack(vp_l)
    wkv_prompt = jnp.stack(wp_l)
    shift_prompt = jnp.stack(sp_l)
    k_sample = jnp.stack(ks_l)
    v_sample = jnp.stack(vs_l)
    wkv_sample = jnp.stack(ws_l)
    shift_sample = jnp.stack(ss_l)
    return (y_prompt, y_sample, k_prompt, v_prompt, wkv_prompt, shift_prompt,
            k_sample, v_sample, wkv_sample, shift_sample)
```

```python
import functools
import math

import jax
import jax.numpy as jnp
import numpy as np
from jax import lax
from jax.experimental import pallas as pl
from jax.experimental.pallas import tpu as pltpu

F32 = jnp.float32
BF16 = jnp.bfloat16

DA_QK = 64
DA_V = 2 * DA_QK
RW_HEAD = 64
RW_W_LORA = 64
RW_A_LORA = 64
RW_G_LORA = 128
RW_GN_EPS = 64e-5
ROPE_THETA = 10000.0
LN_EPS = 1e-5
NEG = -1e30
PAGE_SIZE = 128

V7X_LANES = 128
V7X_VMEM_LIMIT_BYTES = 56 * 1024 * 1024

RW_CHUNK = 64
RW_GROUP = 4
RW_NG = RW_CHUNK * RW_GROUP


def _cparams(semantics):
    return pltpu.CompilerParams(dimension_semantics=semantics,
                                vmem_limit_bytes=V7X_VMEM_LIMIT_BYTES)


def _dot(a, b):
    return jnp.dot(a.astype(BF16), b.astype(BF16), preferred_element_type=F32)


def _dot_nt(a, b):
    return lax.dot_general(a.astype(BF16), b.astype(BF16), (((1,), (1,)), ((), ())),
                           preferred_element_type=F32)


def _split(x):
    hi = x.astype(BF16)
    lo = (x - hi.astype(F32)).astype(BF16)
    return hi, lo


def _dot3(a, b):
    ah, al = _split(a)
    bh, bl = _split(b)
    d = functools.partial(jnp.dot, preferred_element_type=F32)
    return d(ah, bh) + d(ah, bl) + d(al, bh)


def _dot3_nt(a, b):
    ah, al = _split(a)
    bh, bl = _split(b)
    d = functools.partial(lax.dot_general, dimension_numbers=(((1,), (1,)), ((), ())),
                          preferred_element_type=F32)
    return d(ah, bh) + d(ah, bl) + d(al, bh)


def _dot_exact_rhs(a, b):
    ah, al = _split(a)
    return (jnp.dot(ah, b, preferred_element_type=F32)
            + jnp.dot(al, b, preferred_element_type=F32))


def _layernorm(y, g, b):
    mu = jnp.mean(y, axis=-1, keepdims=True)
    d = y - mu
    var = jnp.mean(d * d, axis=-1, keepdims=True)
    return d * lax.rsqrt(var + LN_EPS) * g + b


def _ffn_ln_kernel(x_ref, wg_ref, wu_ref, wd_ref, g_ref, b_ref, o_ref, acc_ref, *, nf, alpha):
    j = pl.program_id(1)
    xb = x_ref[...].astype(BF16)
    gate = jnp.dot(xb, wg_ref[...], preferred_element_type=F32)
    up = jnp.dot(xb, wu_ref[...], preferred_element_type=F32)
    act = (gate * jax.nn.sigmoid(gate) * up).astype(BF16)
    part = jnp.dot(act, wd_ref[...], preferred_element_type=F32)

    @pl.when(j == 0)
    def _():
        acc_ref[...] = part

    @pl.when(j > 0)
    def _():
        acc_ref[...] += part

    @pl.when(j == nf - 1)
    def _():
        y = alpha * x_ref[...] + 0.5 * acc_ref[...]
        o_ref[...] = _layernorm(y, g_ref[...], b_ref[...])


def _ffn_ln(x, w_gu, w_down, g, b, *, alpha, tm, tf):
    m, d = x.shape
    f = w_down.shape[0]
    nf = f // tf
    assert m % tm == 0 and f % tf == 0
    return pl.pallas_call(
        functools.partial(_ffn_ln_kernel, nf=nf, alpha=alpha),
        grid=(m // tm, nf),
        in_specs=[
            pl.BlockSpec((tm, d), lambda i, j: (i, 0)),
            pl.BlockSpec((d, tf), lambda i, j: (0, j)),
            pl.BlockSpec((d, tf), lambda i, j: (0, j + nf)),
            pl.BlockSpec((tf, d), lambda i, j: (j, 0)),
            pl.BlockSpec((1, d), lambda i, j: (0, 0)),
            pl.BlockSpec((1, d), lambda i, j: (0, 0)),
        ],
        out_specs=pl.BlockSpec((tm, d), lambda i, j: (i, 0)),
        out_shape=jax.ShapeDtypeStruct((m, d), F32),
        scratch_shapes=[pltpu.VMEM((tm, d), F32)],
        compiler_params=_cparams(("parallel", "arbitrary")),
        name="ffn_ln",
    )(x, w_gu, w_gu, w_down, g, b)


def _inproj_kernel(h_ref, w_ref, cos_ref, sin_ref,
                   q_ref, k_ref, kb_ref, v_ref, vb_ref, prw_ref, *, n_heads, scale):
    hb = h_ref[...].astype(BF16)
    cos = cos_ref[...]
    sin = sin_ref[...]
    lane = lax.broadcasted_iota(jnp.int32, cos.shape, 1)
    first_half = (lane % DA_QK) < (DA_QK // 2)
    qc = n_heads * 2 * DA_QK

    def rope_head(x):
        rot = jnp.where(first_half, pltpu.roll(x, V7X_LANES - DA_QK // 2, 1),
                        pltpu.roll(x, DA_QK // 2, 1))
        return x * cos + rot * sin

    q = jnp.dot(hb, w_ref[:, 0:qc], preferred_element_type=F32)
    k = jnp.dot(hb, w_ref[:, qc:2 * qc], preferred_element_type=F32)
    for h in range(n_heads):
        sl = slice(h * V7X_LANES, (h + 1) * V7X_LANES)
        qh = rope_head(q[:, sl])
        kh = rope_head(k[:, sl])
        q_ref[:, sl] = (qh * scale).astype(BF16)
        k_ref[:, sl] = kh
        kb_ref[:, sl] = kh.astype(BF16)
    vc = n_heads * DA_V
    v = jnp.dot(hb, w_ref[:, 2 * qc:2 * qc + vc], preferred_element_type=F32)
    v_ref[...] = v
    vb_ref[...] = v.astype(BF16)
    prw_ref[...] = jnp.dot(hb, w_ref[:, 2 * qc + vc:], preferred_element_type=F32)


def _inproj(h, w_in, cos_tab, sin_tab, *, n_heads, tm):
    m, d = h.shape
    cols = w_in.shape[1]
    qc = n_heads * 2 * DA_QK
    vc = n_heads * DA_V
    rw = cols - 2 * qc - vc
    n_pos_blocks = cos_tab.shape[0] // tm
    tok = lambda i: (i, 0)
    pos = lambda i: (i % n_pos_blocks, 0)
    return pl.pallas_call(
        functools.partial(_inproj_kernel, n_heads=n_heads, scale=DA_QK ** -0.5),
        grid=(m // tm,),
        in_specs=[
            pl.BlockSpec((tm, d), tok),
            pl.BlockSpec((d, cols), lambda i: (0, 0)),
            pl.BlockSpec((tm, V7X_LANES), pos),
            pl.BlockSpec((tm, V7X_LANES), pos),
        ],
        out_specs=[
            pl.BlockSpec((tm, qc), tok), pl.BlockSpec((tm, qc), tok), pl.BlockSpec((tm, qc), tok),
            pl.BlockSpec((tm, vc), tok), pl.BlockSpec((tm, vc), tok), pl.BlockSpec((tm, rw), tok),
        ],
        out_shape=[
            jax.ShapeDtypeStruct((m, qc), BF16), jax.ShapeDtypeStruct((m, qc), F32),
            jax.ShapeDtypeStruct((m, qc), BF16), jax.ShapeDtypeStruct((m, vc), F32),
            jax.ShapeDtypeStruct((m, vc), BF16), jax.ShapeDtypeStruct((m, rw), F32),
        ],
        compiler_params=_cparams(("parallel",)),
        name="inproj_rope",
    )(h, w_in, cos_tab, sin_tab)


def _rope_tables(positions):
    half = DA_QK // 2
    inv = ROPE_THETA ** (-jnp.arange(half, dtype=F32) / half)
    ang = positions.astype(F32)[:, None] * inv[None, :]
    cos = jnp.cos(ang)
    sin = jnp.sin(ang)
    cos_tab = jnp.concatenate([cos, cos, cos, cos], axis=1)
    sin_tab = jnp.concatenate([-sin, sin, -sin, sin], axis=1)
    return cos_tab, sin_tab


def _lambda(lq1_ref, lk1_ref, lq2_ref, lk2_ref, lam_init):
    s1 = jnp.sum(lq1_ref[...] * lk1_ref[...], axis=-1, keepdims=True)
    s2 = jnp.sum(lq2_ref[...] * lk2_ref[...], axis=-1, keepdims=True)
    return jnp.exp(s1) - jnp.exp(s2) + lam_init


def _subln(o, g, lam_init):
    return o * lax.rsqrt(jnp.mean(o * o, axis=-1, keepdims=True) + LN_EPS) * g * (1.0 - lam_init)


def _flash_update(carry, s1, s2, vblk):
    m1, l1, a1, m2, l2, a2 = carry
    n1 = jnp.maximum(m1, jnp.max(s1, axis=1, keepdims=True))
    n2 = jnp.maximum(m2, jnp.max(s2, axis=1, keepdims=True))
    p1 = jnp.exp(s1 - n1)
    p2 = jnp.exp(s2 - n2)
    c1 = jnp.exp(m1 - n1)
    c2 = jnp.exp(m2 - n2)
    l1 = c1 * l1 + jnp.sum(p1, axis=1, keepdims=True)
    l2 = c2 * l2 + jnp.sum(p2, axis=1, keepdims=True)
    a1 = c1 * a1 + jnp.dot(p1.astype(BF16), vblk, preferred_element_type=F32)
    a2 = c2 * a2 + jnp.dot(p2.astype(BF16), vblk, preferred_element_type=F32)
    return n1, l1, a1, n2, l2, a2


def _attn_prompt_kernel(lq1_ref, lk1_ref, lq2_ref, lk2_ref, g_ref, q_ref, k_ref, v_ref, o_ref,
                        *, tq, lam_init):
    i = pl.program_id(2)
    q = q_ref[0]
    lane = lax.broadcasted_iota(jnp.int32, q.shape, 1)
    zero = jnp.zeros_like(q)
    qa = jnp.where(lane < DA_QK, q, zero)
    qb = jnp.where(lane >= DA_QK, q, zero)

    def scores(kblk):
        dn = (((1,), (1,)), ((), ()))
        return (lax.dot_general(qa, kblk, dn, preferred_element_type=F32),
                lax.dot_general(qb, kblk, dn, preferred_element_type=F32))

    def body(j, carry):
        off = pl.multiple_of(j * tq, tq)
        kblk = k_ref[0, pl.ds(off, tq), :]
        vblk = v_ref[0, pl.ds(off, tq), :]
        s1, s2 = scores(kblk)
        return _flash_update(carry, s1, s2, vblk)

    col0 = jnp.full((tq, 1), NEG, F32)
    zl = jnp.zeros((tq, 1), F32)
    za = jnp.zeros((tq, DA_V), F32)
    carry = lax.fori_loop(0, i, body, (col0, zl, za, col0, zl, za))

    off = pl.multiple_of(i * tq, tq)
    kblk = k_ref[0, pl.ds(off, tq), :]
    vblk = v_ref[0, pl.ds(off, tq), :]
    s1, s2 = scores(kblk)
    row = lax.broadcasted_iota(jnp.int32, (tq, tq), 0)
    col = lax.broadcasted_iota(jnp.int32, (tq, tq), 1)
    causal = col <= row
    s1 = jnp.where(causal, s1, NEG)
    s2 = jnp.where(causal, s2, NEG)
    _, l1, a1, _, l2, a2 = _flash_update(carry, s1, s2, vblk)

    lam = _lambda(lq1_ref, lk1_ref, lq2_ref, lk2_ref, lam_init)
    o = a1 / l1 - lam * (a2 / l2)
    o_ref[0] = _subln(o, g_ref[...], lam_init).astype(o_ref.dtype)


def _attn_prompt(q, kb, vb, lq1, lk1, lq2, lk2, subln_g, *, n_heads, tq, lam_init):
    b, s, _ = q.shape
    vec = lambda n: pl.BlockSpec((1, n), lambda bi, h, i: (0, 0))
    return pl.pallas_call(
        functools.partial(_attn_prompt_kernel, tq=tq, lam_init=lam_init),
        grid=(b, n_heads, s // tq),
        in_specs=[
            vec(DA_QK), vec(DA_QK), vec(DA_QK), vec(DA_QK), vec(DA_V),
            pl.BlockSpec((1, tq, V7X_LANES), lambda bi, h, i: (bi, i, h)),
            pl.BlockSpec((1, s, V7X_LANES), lambda bi, h, i: (bi, 0, h)),
            pl.BlockSpec((1, s, DA_V), lambda bi, h, i: (bi, 0, h)),
        ],
        out_specs=pl.BlockSpec((1, tq, DA_V), lambda bi, h, i: (bi, i, h)),
        out_shape=jax.ShapeDtypeStruct((b, s, n_heads * DA_V), BF16),
        compiler_params=_cparams(("parallel", "parallel", "arbitrary")),
        name="attn_prompt",
    )(lq1, lk1, lq2, lk2, subln_g, q, kb, vb)


def _attn_decode_kernel(pt_ref, lq1_ref, lk1_ref, lq2_ref, lk2_ref, g_ref,
                        q_ref, kn_ref, vn_ref, kp_ref, vp_ref, o_ref,
                        m_ref, l_ref, acc_ref, *, n_heads, n_pages, lam_init):
    del pt_ref
    p = pl.program_id(1)
    rows = 2 * n_heads
    width = n_heads * V7X_LANES
    q = q_ref[0]
    r_id = lax.broadcasted_iota(jnp.int32, (rows, width), 0)
    l_id = lax.broadcasted_iota(jnp.int32, (rows, width), 1)
    qmat = jnp.where(l_id // DA_QK == r_id, jnp.broadcast_to(q.astype(F32), (rows, width)), 0.0)

    @pl.when(p == 0)
    def _():
        kn = kn_ref[0].astype(F32)
        s_new = jnp.sum(qmat * kn, axis=1, keepdims=True)
        m_ref[...] = s_new
        l_ref[...] = jnp.ones_like(s_new)
        acc_ref[...] = jnp.broadcast_to(vn_ref[0].astype(F32), (rows, width))

    kpage = kp_ref[0].astype(BF16)
    vpage = vp_ref[0].astype(BF16)
    s = lax.dot_general(qmat.astype(BF16), kpage, (((1,), (1,)), ((), ())),
                        preferred_element_type=F32)
    m_old = m_ref[...]
    m_new = jnp.maximum(m_old, jnp.max(s, axis=1, keepdims=True))
    pr = jnp.exp(s - m_new)
    c = jnp.exp(m_old - m_new)
    l_ref[...] = c * l_ref[...] + jnp.sum(pr, axis=1, keepdims=True)
    acc_ref[...] = c * acc_ref[...] + jnp.dot(pr.astype(BF16), vpage, preferred_element_type=F32)
    m_ref[...] = m_new

    @pl.when(p == n_pages - 1)
    def _():
        lam = _lambda(lq1_ref, lk1_ref, lq2_ref, lk2_ref, lam_init)
        on = acc_ref[...] / l_ref[...]
        for h in range(n_heads):
            sl = slice(h * V7X_LANES, (h + 1) * V7X_LANES)
            o = on[2 * h:2 * h + 1, sl] - lam * on[2 * h + 1:2 * h + 2, sl]
            o_ref[0, :, sl] = _subln(o, g_ref[...], lam_init).astype(o_ref.dtype)


def _attn_decode(q, kb, vb, cache_k, cache_v, page_table, lq1, lk1, lq2, lk2, subln_g,
                 *, n_heads, lam_init):
    db, width = q.shape
    n_pages = page_table.shape[1]
    page = cache_k.shape[1]
    rows = 2 * n_heads
    vec = lambda n: pl.BlockSpec((1, n), lambda bi, p, pt: (0, 0))
    tok = pl.BlockSpec((1, 1, width), lambda bi, p, pt: (bi, 0, 0))
    pg = pl.BlockSpec((1, page, width), lambda bi, p, pt: (pt[bi, p], 0, 0))
    grid_spec = pltpu.PrefetchScalarGridSpec(
        num_scalar_prefetch=1,
        grid=(db, n_pages),
        in_specs=[vec(DA_QK), vec(DA_QK), vec(DA_QK), vec(DA_QK), vec(DA_V), tok, tok, tok, pg, pg],
        out_specs=tok,
        scratch_shapes=[pltpu.VMEM((rows, 1), F32), pltpu.VMEM((rows, 1), F32),
                        pltpu.VMEM((rows, width), F32)],
    )
    out = pl.pallas_call(
        functools.partial(_attn_decode_kernel, n_heads=n_heads, n_pages=n_pages, lam_init=lam_init),
        grid_spec=grid_spec,
        out_shape=jax.ShapeDtypeStruct((db, 1, width), BF16),
        compiler_params=_cparams(("parallel", "arbitrary")),
        name="attn_decode",
    )(page_table, lq1, lk1, lq2, lk2, subln_g,
      q.reshape(db, 1, width), kb.reshape(db, 1, width), vb.reshape(db, 1, width), cache_k, cache_v)
    return out.reshape(db, width)


def _rwkv_prep_math(x, prev, mu, w0, w2p, a0, a2p, g2, k_k, k_a, r_k, ones_bd):
    rd = ones_bd.shape[0]
    xm = x + (prev - x) * mu
    r = xm[:, 0:rd]
    k = xm[:, rd:2 * rd]
    v = xm[:, 2 * rd:3 * rd]
    xwa = xm[:, 3 * rd:3 * rd + RW_W_LORA + RW_A_LORA]
    xg = xm[:, 3 * rd + RW_W_LORA + RW_A_LORA:]
    zw = -(w0 + _dot(jnp.tanh(xwa), w2p))
    softplus = jnp.maximum(zw, 0.0) + jnp.log(1.0 + jnp.exp(-jnp.abs(zw)))
    logw = -jnp.exp(-softplus - 0.5)
    a = jax.nn.sigmoid(a0 + _dot(xwa, a2p))
    g = _dot(jax.nn.sigmoid(xg), g2)
    kk = k * k_k
    n2 = _dot_exact_rhs(kk * kk, ones_bd)
    kk = kk / jnp.maximum(jnp.sqrt(n2), 1e-12)
    k2 = k * (1.0 + (a - 1.0) * k_a)
    bonus = _dot_exact_rhs(r * k2 * r_k, ones_bd) * v
    return r, k2, v, -kk, kk * a, logw, g, bonus


def _rwkv_prep_seq_kernel(x_ref, shift_ref, mu_ref, w0_ref, w2p_ref, a0_ref, a2p_ref, g2_ref,
                          kk_ref, ka_ref, rk_ref, ones_ref, *rest):
    outs, carry_ref = rest[:-1], rest[-1]
    t = pl.program_id(1)

    @pl.when(t == 0)
    def _():
        carry_ref[...] = shift_ref[0]

    x = x_ref[0]
    tt = x.shape[0]
    row = lax.broadcasted_iota(jnp.int32, x.shape, 0)
    prev = jnp.where(row == 0, carry_ref[...], pltpu.roll(x, 1, 0))
    carry_ref[...] = x[tt - 1:tt, :]
    vals = _rwkv_prep_math(x, prev, mu_ref[...], w0_ref[...], w2p_ref[...], a0_ref[...],
                           a2p_ref[...], g2_ref[...], kk_ref[...], ka_ref[...], rk_ref[...],
                           ones_ref[...])
    for o_ref, val in zip(outs, vals):
        o_ref[0] = val


def _rwkv_prep_tok_kernel(x_ref, prev_ref, mu_ref, w0_ref, w2p_ref, a0_ref, a2p_ref, g2_ref,
                          kk_ref, ka_ref, rk_ref, ones_ref, *outs):
    vals = _rwkv_prep_math(x_ref[...], prev_ref[...], mu_ref[...], w0_ref[...], w2p_ref[...],
                           a0_ref[...], a2p_ref[...], g2_ref[...], kk_ref[...], ka_ref[...],
                           rk_ref[...], ones_ref[...])
    for o_ref, val in zip(outs, vals):
        o_ref[...] = val


def _rwkv_weights(p, rd):
    zeros_w = jnp.zeros((RW_A_LORA, rd), F32)
    zeros_a = jnp.zeros((RW_W_LORA, rd), F32)
    return [p['rw_mu'][None, :], p['rw_w0'][None, :],
            jnp.concatenate([p['rw_w2'], zeros_w], axis=0).astype(BF16), p['rw_a0'][None, :],
            jnp.concatenate([zeros_a, p['rw_a2']], axis=0).astype(BF16), p['rw_g2'].astype(BF16),
            p['rw_k_k'][None, :], p['rw_k_a'][None, :], p['rw_r_k'].reshape(1, rd)]


def _weight_specs(weights, nargs):
    zero = {1: lambda i: (0, 0), 2: lambda i, j: (0, 0)}[nargs]
    return [pl.BlockSpec(w.shape, zero) for w in weights]


def _rwkv_prep_seq(prw, shift, weights, ones_bd, *, tt):
    b, s, c = prw.shape
    rd = ones_bd.shape[0]
    consts = weights + [ones_bd]
    out_spec = pl.BlockSpec((1, tt, rd), lambda bi, t: (bi, t, 0))
    return pl.pallas_call(
        _rwkv_prep_seq_kernel,
        grid=(b, s // tt),
        in_specs=[pl.BlockSpec((1, tt, c), lambda bi, t: (bi, t, 0)),
                  pl.BlockSpec((1, 1, c), lambda bi, t: (bi, 0, 0))] + _weight_specs(consts, 2),
        out_specs=[out_spec] * 8,
        out_shape=[jax.ShapeDtypeStruct((b, s, rd), F32)] * 8,
        scratch_shapes=[pltpu.VMEM((1, c), F32)],
        compiler_params=_cparams(("parallel", "arbitrary")),
        name="rwkv_prep_seq",
    )(prw, shift, *consts)


def _rwkv_prep_tok(prw, prev, weights, ones_bd):
    m, c = prw.shape
    rd = ones_bd.shape[0]
    consts = weights + [ones_bd]
    return pl.pallas_call(
        _rwkv_prep_tok_kernel,
        grid=(1,),
        in_specs=[pl.BlockSpec((m, c), lambda i: (0, 0))] * 2 + _weight_specs(consts, 1),
        out_specs=[pl.BlockSpec((m, rd), lambda i: (0, 0))] * 8,
        out_shape=[jax.ShapeDtypeStruct((m, rd), F32)] * 8,
        compiler_params=_cparams(("arbitrary",)),
        name="rwkv_prep_tok",
    )(prw, prev, *consts)


def _rwkv_masks():
    n = RW_NG
    row = np.arange(n)[:, None]
    col = np.arange(n)[None, :]
    same = (row // RW_CHUNK) == (col // RW_CHUNK)
    masks = [same & (col < row), same & (col <= row), (row // 2 == col // 2) & (col < row)]
    s = 2
    while s < RW_CHUNK:
        masks.append(((row // s) % 2 == 1) & ((col // s) == (row // s) - 1))
        s *= 2
    return np.stack(masks).astype(np.float32)


def _rwkv_chunk_group(r, k2, v, na, b, logw, masks_ref, tri, s_old):
    c, ng = r.shape
    lane = lax.broadcasted_iota(jnp.int32, (c, ng), 1)

    def stack(x):
        return jnp.concatenate(
            [jnp.where(lane // RW_HEAD == h, x, 0.0) for h in range(ng // RW_HEAD)], axis=0)

    hi = logw.astype(BF16)
    rem = logw - hi.astype(F32)
    mid = rem.astype(BF16)
    lo = (rem - mid.astype(F32)).astype(BF16)
    d = functools.partial(jnp.dot, preferred_element_type=F32)
    cum = d(tri, hi) + d(tri, mid) + d(tri, lo)
    cum_c = cum[c - 1:c, :]
    g_in = jnp.exp(cum)
    g_ex = jnp.exp(cum - logw)
    g_inv = jnp.exp(-cum)
    g_tail = jnp.exp(cum_c - cum)

    at = stack(na * g_ex)
    rt = stack(r * g_in)
    bt = stack(b * g_inv)
    kt = stack(k2 * g_inv)
    bh = stack(b * g_tail)
    kh = stack(k2 * g_tail)
    vm = stack(v)

    gram = _dot_nt(jnp.concatenate([at, rt], axis=0), jnp.concatenate([bt, kt], axis=0))
    strict = masks_ref[0]
    incl = masks_ref[1]
    n = gram[:ng, :ng] * strict
    gak = gram[:ng, ng:] * strict
    grb = gram[ng:, :ng] * incl
    grk = gram[ng:, ng:] * incl

    row = lax.broadcasted_iota(jnp.int32, (ng, ng), 0)
    col = lax.broadcasted_iota(jnp.int32, (ng, ng), 1)
    x = jnp.where(row == col, 1.0, 0.0) + n * masks_ref[2]
    for lvl in range(3, masks_ref.shape[0]):
        x = x + _dot(_dot(x, n * masks_ref[lvl]), x)

    au = _dot(x, jnp.concatenate([at, _dot(gak, vm)], axis=1))
    ab = au[:, :ng]
    ub = au[:, ng:]
    t = _dot(grb, au)
    rb = rt + t[:, :ng]
    ob = t[:, ng:] + _dot(grk, vm)
    m = jnp.where(row == col, jnp.exp(cum_c), 0.0) + _dot(ab.T, bh)
    z = _dot(jnp.concatenate([ub, vm], axis=0).T, jnp.concatenate([bh, kh], axis=0))

    o = _dot3_nt(rb, s_old) + ob
    s_new = _dot3(s_old, m) + z
    o_tok = o[0:c]
    for h in range(1, ng // RW_HEAD):
        o_tok = o_tok + o[h * c:(h + 1) * c]
    return o_tok, s_new


def _rwkv_chunk_kernel(r_ref, k_ref, v_ref, na_ref, b_ref, lw_ref, masks_ref, tri_ref,
                       o_ref, sout_ref, s_ref, *, n_groups, n_chunks):
    ci = pl.program_id(1)

    @pl.when(ci == 0)
    def _():
        s_ref[...] = jnp.zeros_like(s_ref)

    tri = tri_ref[...]
    for gi in range(n_groups):
        sl = slice(gi * RW_NG, (gi + 1) * RW_NG)
        o_tok, s_new = _rwkv_chunk_group(r_ref[0, :, sl], k_ref[0, :, sl], v_ref[0, :, sl],
                                         na_ref[0, :, sl], b_ref[0, :, sl], lw_ref[0, :, sl],
                                         masks_ref, tri, s_ref[gi])
        o_ref[0, :, sl] = o_tok
        s_ref[gi] = s_new

    @pl.when(ci == n_chunks - 1)
    def _():
        for gi in range(n_groups):
            s = s_ref[gi]
            for h in range(RW_GROUP):
                hs = slice(h * RW_HEAD, (h + 1) * RW_HEAD)
                sout_ref[0, gi * RW_GROUP + h] = s[hs, hs]


def _rwkv_chunked(r, k2, v, na, b, logw):
    bsz, s, rd = r.shape
    n_groups = rd // RW_NG
    n_chunks = s // RW_CHUNK
    n_heads = rd // RW_HEAD
    masks = jnp.asarray(_rwkv_masks())
    tri = jnp.asarray(np.tril(np.ones((RW_CHUNK, RW_CHUNK), np.float32))).astype(BF16)
    tok = pl.BlockSpec((1, RW_CHUNK, rd), lambda bi, ci: (bi, ci, 0))
    return pl.pallas_call(
        functools.partial(_rwkv_chunk_kernel, n_groups=n_groups, n_chunks=n_chunks),
        grid=(bsz, n_chunks),
        in_specs=[tok] * 6 + [pl.BlockSpec(masks.shape, lambda bi, ci: (0, 0, 0)),
                              pl.BlockSpec(tri.shape, lambda bi, ci: (0, 0))],
        out_specs=[tok, pl.BlockSpec((1, n_heads, RW_HEAD, RW_HEAD), lambda bi, ci: (bi, 0, 0, 0))],
        out_shape=[jax.ShapeDtypeStruct((bsz, s, rd), F32),
                   jax.ShapeDtypeStruct((bsz, n_heads, RW_HEAD, RW_HEAD), F32)],
        scratch_shapes=[pltpu.VMEM((n_groups, RW_NG, RW_NG), F32)],
        compiler_params=_cparams(("parallel", "arbitrary")),
        name="rwkv_chunked",
    )(r, k2, v, na, b, logw, masks, tri)


def _rwkv_step_kernel(s_ref, r_ref, k_ref, na_ref, b_ref, lw_ref, v_ref, sout_ref, o_ref):
    s = s_ref[0]
    sa = jnp.sum(s * na_ref[0], axis=-1, keepdims=True)
    s_new = s * jnp.exp(lw_ref[0]) + sa * b_ref[0] + v_ref[0] * k_ref[0]
    sout_ref[0] = s_new
    o_ref[0] = jnp.sum(s_new * r_ref[0], axis=-1, keepdims=True)


def _rwkv_step(state, r, k2, v, na, b, logw):
    db, nh, hd, _ = state.shape
    rowv = lambda x: x.reshape(db, nh, 1, hd)
    rspec = pl.BlockSpec((1, nh, 1, hd), lambda i: (i, 0, 0, 0))
    cspec = pl.BlockSpec((1, nh, hd, 1), lambda i: (i, 0, 0, 0))
    sspec = pl.BlockSpec((1, nh, hd, hd), lambda i: (i, 0, 0, 0))
    s_new, o = pl.pallas_call(
        _rwkv_step_kernel,
        grid=(db,),
        in_specs=[sspec, rspec, rspec, rspec, rspec, rspec, cspec],
        out_specs=[sspec, cspec],
        out_shape=[jax.ShapeDtypeStruct(state.shape, F32), jax.ShapeDtypeStruct((db, nh, hd, 1), F32)],
        compiler_params=_cparams(("parallel",)),
        name="rwkv_step",
    )(state, rowv(r), rowv(k2), rowv(na), rowv(b), rowv(logw), v.reshape(db, nh, hd, 1))
    return o.reshape(db, nh * hd), s_new


def _outproj_ln_kernel(oda_ref, orw_ref, bonus_ref, gate_ref, h_ref, w_ref, lg_ref, lb_ref,
                       ones_ref, g_ref, b_ref, o_ref, *, alpha):
    o = orw_ref[...]
    ones_bd = ones_ref[...]
    inv_n = 1.0 / RW_HEAD
    mu = _dot_exact_rhs(o, ones_bd) * inv_n
    d = o - mu
    var = _dot_exact_rhs(d * d, ones_bd) * inv_n
    o = d * lax.rsqrt(var + RW_GN_EPS) * lg_ref[...] + lb_ref[...]
    o_rw = (o + bonus_ref[...]) * gate_ref[...]
    da = oda_ref.shape[1]
    mix = (jnp.dot(oda_ref[...], w_ref[0:da, :], preferred_element_type=F32)
           + jnp.dot(o_rw.astype(BF16), w_ref[da:, :], preferred_element_type=F32))
    o_ref[...] = _layernorm(alpha * h_ref[...] + mix, g_ref[...], b_ref[...])


def _outproj_ln(oda, orw, bonus, gate, h, w_out, lnx_g, lnx_b, ones_bd, g, b, *, alpha, tm):
    m, d = h.shape
    da = oda.shape[1]
    rd = orw.shape[1]
    tok = lambda n: pl.BlockSpec((tm, n), lambda i: (i, 0))
    consts = [w_out, lnx_g, lnx_b, ones_bd, g, b]
    return pl.pallas_call(
        functools.partial(_outproj_ln_kernel, alpha=alpha),
        grid=(m // tm,),
        in_specs=[tok(da), tok(rd), tok(rd), tok(rd), tok(d)] + _weight_specs(consts, 1),
        out_specs=tok(d),
        out_shape=jax.ShapeDtypeStruct((m, d), F32),
        compiler_params=_cparams(("parallel",)),
        name="outproj_ln",
    )(oda, orw, bonus, gate, h, *consts)


def _tile(m, pref):
    return pref if m % pref == 0 else m


def kernel(x_prompt, x_sample, cache_k, cache_v, state_wkv, state_shift, page_table, ffa_w_gu, ffa_w_down, ffb_w_gu, ffb_w_down, ln1_g, ln1_b, ln2_g, ln2_b, ln3_g, ln3_b, w_in, w_out, da_lq1, da_lk1, da_lq2, da_lk2, da_subln_g, rw_mu, rw_w0, rw_w2, rw_a0, rw_a2, rw_g2, rw_k_k, rw_k_a, rw_r_k, rw_lnx_g, rw_lnx_b):
    bsz, seq, d_model = x_prompt.shape
    db, dec_seq, _ = x_sample.shape
    assert dec_seq == 1, "the sample group is written for one new token per sequence"
    depth = w_in.shape[0]
    n_pool, page, n_heads, _ = cache_k.shape[1:]
    n_pages = page_table.shape[1]
    rd = rw_w0.shape[1]
    rw_heads = rd // RW_HEAD
    d_ff = ffa_w_down.shape[1]
    alpha = (2 * depth) ** 0.25
    width = n_heads * V7X_LANES

    cos_p, sin_p = _rope_tables(jnp.arange(seq))
    cos_s, sin_s = _rope_tables(jnp.full((db,), n_pages * page, jnp.int32))
    ones_bd = jnp.asarray(np.kron(np.eye(rw_heads, dtype=np.float32),
                                  np.ones((RW_HEAD, RW_HEAD), np.float32))).astype(BF16)

    m_p = bsz * seq
    tm_p = _tile(m_p, 512)
    tf = _tile(d_ff, 1408)
    y_p = x_prompt.reshape(m_p, d_model)
    y_s = x_sample.reshape(db, d_model)
    outs = [[] for _ in range(8)]
    row = lambda a: a[None, :]

    for l in range(depth):
        lam_init = 0.8 - 0.6 * math.exp(-0.3 * l)
        wa_gu, wa_dn = ffa_w_gu[l].astype(BF16), ffa_w_down[l].astype(BF16)
        wb_gu, wb_dn = ffb_w_gu[l].astype(BF16), ffb_w_down[l].astype(BF16)
        wi, wo = w_in[l].astype(BF16), w_out[l].astype(BF16)
        lam_vecs = [row(da_lq1[l]), row(da_lk1[l]), row(da_lq2[l]), row(da_lk2[l]), row(da_subln_g[l])]
        rw_weights = _rwkv_weights(dict(rw_mu=rw_mu[l], rw_w0=rw_w0[l], rw_w2=rw_w2[l], rw_a0=rw_a0[l],
                                        rw_a2=rw_a2[l], rw_g2=rw_g2[l], rw_k_k=rw_k_k[l],
                                        rw_k_a=rw_k_a[l], rw_r_k=rw_r_k[l]), rd)
        ln = [row(a[l]) for a in (ln1_g, ln1_b, ln2_g, ln2_b, ln3_g, ln3_b)]
        lnx = [row(rw_lnx_g[l]), row(rw_lnx_b[l])]

        h1 = _ffn_ln(y_p, wa_gu, wa_dn, ln[0], ln[1], alpha=alpha, tm=tm_p, tf=tf)
        q, k, kb, v, vb, prw = _inproj(h1, wi, cos_p, sin_p, n_heads=n_heads, tm=tm_p)
        oda = _attn_prompt(q.reshape(bsz, seq, width), kb.reshape(bsz, seq, width),
                           vb.reshape(bsz, seq, width), *lam_vecs,
                           n_heads=n_heads, tq=256, lam_init=lam_init)
        prw3 = prw.reshape(bsz, seq, -1)
        shift0 = jnp.zeros((bsz, 1, prw3.shape[-1]), F32)
        r_, k2_, v_, na_, b_, lw_, gate, bonus = _rwkv_prep_seq(prw3, shift0, rw_weights, ones_bd, tt=512)
        orw, wkv_p = _rwkv_chunked(r_, k2_, v_, na_, b_, lw_)
        h2 = _outproj_ln(oda.reshape(m_p, width), orw.reshape(m_p, rd), bonus.reshape(m_p, rd),
                         gate.reshape(m_p, rd), h1, wo, *lnx, ones_bd, ln[2], ln[3],
                         alpha=alpha, tm=tm_p)
        y_p = _ffn_ln(h2, wb_gu, wb_dn, ln[4], ln[5], alpha=alpha, tm=tm_p, tf=tf)
        outs[0].append(k.reshape(bsz, seq, n_heads, 2 * DA_QK))
        outs[1].append(v.reshape(bsz, seq, n_heads, DA_V))
        outs[2].append(wkv_p)
        outs[3].append(prw3[:, seq - 1:seq, :])

        h1 = _ffn_ln(y_s, wa_gu, wa_dn, ln[0], ln[1], alpha=alpha, tm=db, tf=tf)
        q, k, kb, v, vb, prw = _inproj(h1, wi, cos_s, sin_s, n_heads=n_heads, tm=db)
        oda = _attn_decode(q, kb, vb, cache_k[l].reshape(n_pool, page, width),
                           cache_v[l].reshape(n_pool, page, width), page_table, *lam_vecs,
                           n_heads=n_heads, lam_init=lam_init)
        r_, k2_, v_, na_, b_, lw_, gate, bonus = _rwkv_prep_tok(
            prw, state_shift[l].reshape(db, -1), rw_weights, ones_bd)
        orw, wkv_s = _rwkv_step(state_wkv[l], r_, k2_, v_, na_, b_, lw_)
        h2 = _outproj_ln(oda, orw, bonus, gate, h1, wo, *lnx, ones_bd, ln[2], ln[3],
                         alpha=alpha, tm=db)
        y_s = _ffn_ln(h2, wb_gu, wb_dn, ln[4], ln[5], alpha=alpha, tm=db, tf=tf)
        outs[4].append(k.reshape(db, 1, n_heads, 2 * DA_QK))
        outs[5].append(v.reshape(db, 1, n_heads, DA_V))
        outs[6].append(wkv_s)
        outs[7].append(prw.reshape(db, 1, -1))

    stacked = [jnp.stack(o) for o in outs]
    return (y_p.reshape(bsz, seq, d_model), y_s.reshape(db, 1, d_model), *stacked)
```

```python
import functools
import math

import jax
import jax.numpy as jnp
import numpy as np
from jax import lax
from jax.experimental import pallas as pl
from jax.experimental.pallas import tpu as pltpu

F32 = jnp.float32
BF16 = jnp.bfloat16

DA_QK = 64
DA_V = 2 * DA_QK
RW_HEAD = 64
RW_W_LORA = 64
RW_A_LORA = 64
RW_G_LORA = 128
RW_GN_EPS = 64e-5
ROPE_THETA = 10000.0
LN_EPS = 1e-5
NEG = -1e30
LOG2E = math.log2(math.e)

V7X_LANES = 128
V7X_VMEM_LIMIT_BYTES = 56 * 1024 * 1024

RW_CHUNK = 64
RW_GROUP = 4
RW_NG = RW_CHUNK * RW_GROUP


def _cparams(semantics):
    return pltpu.CompilerParams(dimension_semantics=semantics,
                                vmem_limit_bytes=V7X_VMEM_LIMIT_BYTES)


def _dot(a, b):
    return jnp.dot(a.astype(BF16), b.astype(BF16), preferred_element_type=F32)


def _dot_nt(a, b):
    return lax.dot_general(a.astype(BF16), b.astype(BF16), (((1,), (1,)), ((), ())),
                           preferred_element_type=F32)


def _split(x):
    hi = x.astype(BF16)
    lo = (x - hi.astype(F32)).astype(BF16)
    return hi, lo


def _dot_exact_rhs(a, b):
    ah, al = _split(a)
    return (jnp.dot(ah, b, preferred_element_type=F32)
            + jnp.dot(al, b, preferred_element_type=F32))


def _layernorm(y, g, b):
    mu = jnp.mean(y, axis=-1, keepdims=True)
    d = y - mu
    var = jnp.mean(d * d, axis=-1, keepdims=True)
    return d * lax.rsqrt(var + LN_EPS) * g + b


def _ffn_ln_kernel(x_ref, wg_ref, wu_ref, wd_ref, g_ref, b_ref, o_ref, acc_ref, *, nf, alpha):
    j = pl.program_id(1)
    xb = x_ref[...].astype(BF16)
    gate = jnp.dot(xb, wg_ref[...], preferred_element_type=F32)
    up = jnp.dot(xb, wu_ref[...], preferred_element_type=F32)
    act = (gate * jax.nn.sigmoid(gate) * up).astype(BF16)
    part = jnp.dot(act, wd_ref[...], preferred_element_type=F32)

    @pl.when(j == 0)
    def _():
        acc_ref[...] = part

    @pl.when(j > 0)
    def _():
        acc_ref[...] += part

    @pl.when(j == nf - 1)
    def _():
        y = alpha * x_ref[...] + 0.5 * acc_ref[...]
        o_ref[...] = _layernorm(y, g_ref[...], b_ref[...])


def _ffn_ln(x, w_gu, w_down, g, b, *, alpha, tm, tf):
    m, d = x.shape
    f = w_down.shape[0]
    nf = f // tf
    assert m % tm == 0 and f % tf == 0
    return pl.pallas_call(
        functools.partial(_ffn_ln_kernel, nf=nf, alpha=alpha),
        grid=(m // tm, nf),
        in_specs=[
            pl.BlockSpec((tm, d), lambda i, j: (i, 0)),
            pl.BlockSpec((d, tf), lambda i, j: (0, j)),
            pl.BlockSpec((d, tf), lambda i, j: (0, j + nf)),
            pl.BlockSpec((tf, d), lambda i, j: (j, 0)),
            pl.BlockSpec((1, d), lambda i, j: (0, 0)),
            pl.BlockSpec((1, d), lambda i, j: (0, 0)),
        ],
        out_specs=pl.BlockSpec((tm, d), lambda i, j: (i, 0)),
        out_shape=jax.ShapeDtypeStruct((m, d), F32),
        scratch_shapes=[pltpu.VMEM((tm, d), F32)],
        compiler_params=_cparams(("parallel", "arbitrary")),
        name="ffn_ln",
    )(x, w_gu, w_gu, w_down, g, b)


def _inproj_kernel(h_ref, w_ref, cos_ref, sin_ref,
                   q_ref, k_ref, kb_ref, v_ref, vb_ref, prw_ref, *, n_heads, scale):
    hb = h_ref[...].astype(BF16)
    cos = cos_ref[...]
    sin = sin_ref[...]
    lane = lax.broadcasted_iota(jnp.int32, cos.shape, 1)
    first_half = (lane % DA_QK) < (DA_QK // 2)
    qc = n_heads * 2 * DA_QK

    def rope_head(x):
        rot = jnp.where(first_half, pltpu.roll(x, V7X_LANES - DA_QK // 2, 1),
                        pltpu.roll(x, DA_QK // 2, 1))
        return x * cos + rot * sin

    q = jnp.dot(hb, w_ref[:, 0:qc], preferred_element_type=F32)
    k = jnp.dot(hb, w_ref[:, qc:2 * qc], preferred_element_type=F32)
    for h in range(n_heads):
        sl = slice(h * V7X_LANES, (h + 1) * V7X_LANES)
        qh = rope_head(q[:, sl])
        kh = rope_head(k[:, sl])
        q_ref[:, sl] = (qh * scale).astype(BF16)
        k_ref[:, sl] = kh
        kb_ref[:, sl] = kh.astype(BF16)
    vc = n_heads * DA_V
    v = jnp.dot(hb, w_ref[:, 2 * qc:2 * qc + vc], preferred_element_type=F32)
    v_ref[...] = v
    vb_ref[...] = v.astype(BF16)
    prw_ref[...] = jnp.dot(hb, w_ref[:, 2 * qc + vc:], preferred_element_type=F32)


def _inproj(h, w_in, cos_tab, sin_tab, *, n_heads, tm):
    m, d = h.shape
    cols = w_in.shape[1]
    qc = n_heads * 2 * DA_QK
    vc = n_heads * DA_V
    rw = cols - 2 * qc - vc
    n_pos_blocks = cos_tab.shape[0] // tm
    tok = lambda i: (i, 0)
    pos = lambda i: (i % n_pos_blocks, 0)
    return pl.pallas_call(
        functools.partial(_inproj_kernel, n_heads=n_heads, scale=DA_QK ** -0.5 * LOG2E),
        grid=(m // tm,),
        in_specs=[
            pl.BlockSpec((tm, d), tok),
            pl.BlockSpec((d, cols), lambda i: (0, 0)),
            pl.BlockSpec((tm, V7X_LANES), pos),
            pl.BlockSpec((tm, V7X_LANES), pos),
        ],
        out_specs=[
            pl.BlockSpec((tm, qc), tok), pl.BlockSpec((tm, qc), tok), pl.BlockSpec((tm, qc), tok),
            pl.BlockSpec((tm, vc), tok), pl.BlockSpec((tm, vc), tok), pl.BlockSpec((tm, rw), tok),
        ],
        out_shape=[
            jax.ShapeDtypeStruct((m, qc), BF16), jax.ShapeDtypeStruct((m, qc), F32),
            jax.ShapeDtypeStruct((m, qc), BF16), jax.ShapeDtypeStruct((m, vc), F32),
            jax.ShapeDtypeStruct((m, vc), BF16), jax.ShapeDtypeStruct((m, rw), F32),
        ],
        compiler_params=_cparams(("parallel",)),
        name="inproj_rope",
    )(h, w_in, cos_tab, sin_tab)


def _rope_tables(positions):
    half = DA_QK // 2
    inv = ROPE_THETA ** (-jnp.arange(half, dtype=F32) / half)
    ang = positions.astype(F32)[:, None] * inv[None, :]
    cos = jnp.cos(ang)
    sin = jnp.sin(ang)
    cos_tab = jnp.concatenate([cos, cos, cos, cos], axis=1)
    sin_tab = jnp.concatenate([-sin, sin, -sin, sin], axis=1)
    return cos_tab, sin_tab


def _lambda(lq1_ref, lk1_ref, lq2_ref, lk2_ref, lam_init):
    s1 = jnp.sum(lq1_ref[...] * lk1_ref[...], axis=-1, keepdims=True)
    s2 = jnp.sum(lq2_ref[...] * lk2_ref[...], axis=-1, keepdims=True)
    return jnp.exp(s1) - jnp.exp(s2) + lam_init


def _subln(o, g, lam_init):
    return o * lax.rsqrt(jnp.mean(o * o, axis=-1, keepdims=True) + LN_EPS) * g * (1.0 - lam_init)


def _attn_prompt_kernel(lq1_ref, lk1_ref, lq2_ref, lk2_ref, g_ref, q_ref, k_ref, v_ref, o_ref,
                        qm_ref, s_ref, m_ref, l_ref, acc_ref, *, tq, rs, lam_init):
    i = pl.program_id(2)
    q = q_ref[0].astype(F32)
    lane = lax.broadcasted_iota(jnp.int32, q.shape, 1)
    qm_ref[0] = jnp.where(lane < DA_QK, q, 0.0).astype(BF16)
    qm_ref[1] = jnp.where(lane >= DA_QK, q, 0.0).astype(BF16)
    m_ref[...] = jnp.full(m_ref.shape, NEG, F32)
    l_ref[...] = jnp.zeros(l_ref.shape, F32)
    acc_ref[...] = jnp.zeros(acc_ref.shape, F32)
    nt = tq // V7X_LANES

    def scores(j, buf):
        kblk = k_ref[0, pl.ds(pl.multiple_of(j * tq, tq), tq), :]
        dn = (((1,), (1,)), ((), ()))
        for mp in range(2):
            s_ref[buf, mp] = lax.dot_general(qm_ref[mp], kblk, dn, preferred_element_type=F32)

    def softmax_pv(j, buf, diagonal):
        off = pl.multiple_of(j * tq, tq)
        for r in range(tq // rs):
            rows = slice(r * rs, (r + 1) * rs)
            n_vis = min(nt, pl.cdiv((r + 1) * rs, V7X_LANES)) if diagonal else nt
            vblk = v_ref[0, pl.ds(off, n_vis * V7X_LANES), :]
            for mp in range(2):
                tiles = []
                for t in range(n_vis):
                    s = s_ref[buf, mp, rows, t * V7X_LANES:(t + 1) * V7X_LANES]
                    if diagonal and (t + 1) * V7X_LANES > r * rs + 1:
                        row = lax.broadcasted_iota(jnp.int32, s.shape, 0) + r * rs
                        col = lax.broadcasted_iota(jnp.int32, s.shape, 1) + t * V7X_LANES
                        s = jnp.where(col <= row, s, NEG)
                    tiles.append(s)
                mx = functools.reduce(jnp.maximum, tiles)
                m_old = m_ref[mp, rows, :]
                m_new = jnp.maximum(m_old, jnp.max(mx, axis=1, keepdims=True))
                ps = [jnp.exp2(s - m_new) for s in tiles]
                c = jnp.exp2(m_old - m_new)
                rowsum = jnp.sum(functools.reduce(jnp.add, ps), axis=1, keepdims=True)
                l_ref[mp, rows, :] = c * l_ref[mp, rows, :] + rowsum
                m_ref[mp, rows, :] = m_new
                p = jnp.concatenate([x.astype(BF16) for x in ps], axis=1)
                pv = jnp.dot(p, vblk, preferred_element_type=F32)
                acc_ref[mp, rows, :] = c * acc_ref[mp, rows, :] + pv

    scores(0, 0)

    def body(jj, carry):
        j = 2 * jj
        scores(j + 1, 1)
        softmax_pv(j, 0, False)
        scores(j + 2, 0)
        softmax_pv(j + 1, 1, False)
        return carry

    lax.fori_loop(0, i // 2, body, 0)

    @pl.when(i % 2 == 0)
    def _():
        softmax_pv(i, 0, True)

    @pl.when(i % 2 == 1)
    def _():
        scores(i, 1)
        softmax_pv(i - 1, 0, False)
        softmax_pv(i, 1, True)

    lam = _lambda(lq1_ref, lk1_ref, lq2_ref, lk2_ref, lam_init)
    o = acc_ref[0] / l_ref[0] - lam * (acc_ref[1] / l_ref[1])
    o_ref[0] = _subln(o, g_ref[...], lam_init).astype(o_ref.dtype)


def _attn_prompt(q, kb, vb, lq1, lk1, lq2, lk2, subln_g, *, n_heads, tq, rs, lam_init):
    b, s, _ = q.shape
    vec = lambda n: pl.BlockSpec((1, n), lambda bi, h, i: (0, 0))
    return pl.pallas_call(
        functools.partial(_attn_prompt_kernel, tq=tq, rs=rs, lam_init=lam_init),
        grid=(b, n_heads, s // tq),
        in_specs=[
            vec(DA_QK), vec(DA_QK), vec(DA_QK), vec(DA_QK), vec(DA_V),
            pl.BlockSpec((1, tq, V7X_LANES), lambda bi, h, i: (bi, i, h)),
            pl.BlockSpec((1, s, V7X_LANES), lambda bi, h, i: (bi, 0, h)),
            pl.BlockSpec((1, s, DA_V), lambda bi, h, i: (bi, 0, h)),
        ],
        out_specs=pl.BlockSpec((1, tq, DA_V), lambda bi, h, i: (bi, i, h)),
        out_shape=jax.ShapeDtypeStruct((b, s, n_heads * DA_V), BF16),
        scratch_shapes=[pltpu.VMEM((2, tq, V7X_LANES), BF16), pltpu.VMEM((2, 2, tq, tq), F32),
                        pltpu.VMEM((2, tq, DA_V), F32), pltpu.VMEM((2, tq, DA_V), F32),
                        pltpu.VMEM((2, tq, DA_V), F32)],
        compiler_params=_cparams(("parallel", "parallel", "arbitrary")),
        name="attn_prompt",
    )(lq1, lk1, lq2, lk2, subln_g, q, kb, vb)


def _attn_decode_kernel(pt_ref, lq1_ref, lk1_ref, lq2_ref, lk2_ref, g_ref, q_ref, kn_ref, vn_ref,
                        *rest, n_heads, n_steps, gp, lam_init):
    del pt_ref
    kp_refs, vp_refs = rest[:gp], rest[gp:2 * gp]
    o_ref, qm_ref, ex_ref, m_ref, l_ref, acc_ref = rest[2 * gp:]
    p = pl.program_id(1)
    nj = 2 * n_heads
    width = n_heads * V7X_LANES
    dn = (((1,), (1,)), ((), ()))
    sub = lax.broadcasted_iota(jnp.int32, (8, V7X_LANES), 0)

    @pl.when(p == 0)
    def _():
        e_r = lax.broadcasted_iota(jnp.int32, ex_ref.shape, 0)
        e_c = lax.broadcasted_iota(jnp.int32, ex_ref.shape, 1)
        ex_ref[...] = jnp.where(e_c // V7X_LANES == e_r, 1.0, 0.0).astype(BF16)
        q = q_ref[0].astype(F32)
        r_id = lax.broadcasted_iota(jnp.int32, (V7X_LANES, width), 0)
        l_id = lax.broadcasted_iota(jnp.int32, (V7X_LANES, width), 1)
        qm = jnp.where(l_id // DA_QK == r_id, jnp.broadcast_to(q, (V7X_LANES, width)), 0.0)
        qm_ref[...] = qm.astype(BF16)
        kn = jnp.broadcast_to(kn_ref[0], (8, width))
        s_new = lax.dot_general(kn, qm_ref[...], dn, preferred_element_type=F32)
        m_ref[...] = s_new[0:1, :]
        l_ref[...] = jnp.ones(l_ref.shape, F32)
        vn = vn_ref[0].astype(F32)
        for j in range(nj):
            h = j // 2
            vh = jnp.broadcast_to(vn[:, h * V7X_LANES:(h + 1) * V7X_LANES], (8, V7X_LANES))
            acc_ref[j] = jnp.where(sub == 0, vh, 0.0)

    qm = qm_ref[...]
    page = kp_refs[0].shape[1]
    s_pg = [lax.dot_general(kp_refs[g][0].astype(BF16), qm, dn, preferred_element_type=F32)
            for g in range(gp)]
    m_pg = [jnp.max(s, axis=0, keepdims=True) for s in s_pg]
    pr_pg = [jnp.exp2(s - m_g) for s, m_g in zip(s_pg, m_pg)]
    l_pg = [jnp.sum(pr, axis=0, keepdims=True) for pr in pr_pg]
    ex = ex_ref[...]
    pb_pg = [jnp.dot(pr.astype(BF16), ex, preferred_element_type=F32) for pr in pr_pg]
    part_pg = []
    for g in range(gp):
        parts = []
        for j in range(nj):
            h = j // 2
            pj = pb_pg[g][:, j * V7X_LANES:(j + 1) * V7X_LANES]
            vh = vp_refs[g][0, :, h * V7X_LANES:(h + 1) * V7X_LANES]
            parts.append(jnp.sum((pj * vh).reshape(page // 8, 8, V7X_LANES), axis=0))
        part_pg.append(parts)
    m_old = m_ref[...]
    m_new = functools.reduce(jnp.maximum, m_pg, m_old)
    c = jnp.exp2(m_old - m_new)
    w_pg = [jnp.exp2(m_g - m_new) for m_g in m_pg]
    l_ref[...] = c * l_ref[...] + functools.reduce(
        jnp.add, [w * l_g for w, l_g in zip(w_pg, l_pg)])
    m_ref[...] = m_new
    for j in range(nj):
        acc = jnp.broadcast_to(c[:, j:j + 1], (8, V7X_LANES)) * acc_ref[j]
        for g in range(gp):
            acc = acc + jnp.broadcast_to(w_pg[g][:, j:j + 1], (8, V7X_LANES)) * part_pg[g][j]
        acc_ref[j] = acc

    @pl.when(p == n_steps - 1)
    def _():
        lam = _lambda(lq1_ref, lk1_ref, lq2_ref, lk2_ref, lam_init)
        l = l_ref[...]
        for h in range(n_heads):
            o1 = jnp.sum(acc_ref[2 * h], axis=0, keepdims=True) / l[:, 2 * h:2 * h + 1]
            o2 = jnp.sum(acc_ref[2 * h + 1], axis=0, keepdims=True) / l[:, 2 * h + 1:2 * h + 2]
            o_ref[0, :, h * V7X_LANES:(h + 1) * V7X_LANES] = _subln(
                o1 - lam * o2, g_ref[...], lam_init).astype(o_ref.dtype)


def _attn_decode(q, kb, vb, cache_k, cache_v, page_table, page_base, lq1, lk1, lq2, lk2, subln_g,
                 *, n_heads, gp, lam_init):
    db, width = q.shape
    n_pages = page_table.shape[1]
    page = cache_k.shape[1]
    nj = 2 * n_heads
    n_steps = n_pages // gp
    vec = lambda n: pl.BlockSpec((1, n), lambda bi, p, pt: (0, 0))
    tok = pl.BlockSpec((1, 1, width), lambda bi, p, pt: (bi, 0, 0))
    pages = [pl.BlockSpec((1, page, width),
                          lambda bi, p, pt, g=g: (page_base + pt[bi, p * gp + g], 0, 0))
             for g in range(gp)]
    grid_spec = pltpu.PrefetchScalarGridSpec(
        num_scalar_prefetch=1,
        grid=(db, n_steps),
        in_specs=[vec(DA_QK), vec(DA_QK), vec(DA_QK), vec(DA_QK), vec(DA_V), tok, tok, tok]
                 + pages + pages,
        out_specs=tok,
        scratch_shapes=[pltpu.VMEM((V7X_LANES, width), BF16),
                        pltpu.VMEM((V7X_LANES, nj * V7X_LANES), BF16), pltpu.VMEM((1, V7X_LANES), F32),
                        pltpu.VMEM((1, V7X_LANES), F32), pltpu.VMEM((nj, 8, V7X_LANES), F32)],
    )
    out = pl.pallas_call(
        functools.partial(_attn_decode_kernel, n_heads=n_heads, n_steps=n_steps, gp=gp,
                          lam_init=lam_init),
        grid_spec=grid_spec,
        out_shape=jax.ShapeDtypeStruct((db, 1, width), BF16),
        compiler_params=_cparams(("parallel", "arbitrary")),
        name="attn_decode",
    )(page_table, lq1, lk1, lq2, lk2, subln_g,
      q.reshape(db, 1, width), kb.reshape(db, 1, width), vb.reshape(db, 1, width),
      *([cache_k] * gp), *([cache_v] * gp))
    return out.reshape(db, width)


def _rwkv_prep_math(x, prev, mu, w0, w2p, a0, a2p, g2, k_k, k_a, r_k, ones_bd):
    rd = ones_bd.shape[0]
    xm = x + (prev - x) * mu
    r = xm[:, 0:rd]
    k = xm[:, rd:2 * rd]
    v = xm[:, 2 * rd:3 * rd]
    xwa = xm[:, 3 * rd:3 * rd + RW_W_LORA + RW_A_LORA]
    xg = xm[:, 3 * rd + RW_W_LORA + RW_A_LORA:]
    zw = -(w0 + _dot(jnp.tanh(xwa), w2p))
    softplus = jnp.maximum(zw, 0.0) + jnp.log(1.0 + jnp.exp(-jnp.abs(zw)))
    logw = -jnp.exp(-softplus - 0.5)
    a = jax.nn.sigmoid(a0 + _dot(xwa, a2p))
    g = _dot(jax.nn.sigmoid(xg), g2)
    kk = k * k_k
    n2 = _dot_exact_rhs(kk * kk, ones_bd)
    kk = kk / jnp.maximum(jnp.sqrt(n2), 1e-12)
    k2 = k * (1.0 + (a - 1.0) * k_a)
    bonus = _dot_exact_rhs(r * k2 * r_k, ones_bd) * v
    return r, k2, v, -kk, kk * a, logw, g, bonus


def _rwkv_prep_seq_kernel(x_ref, shift_ref, mu_ref, w0_ref, w2p_ref, a0_ref, a2p_ref, g2_ref,
                          kk_ref, ka_ref, rk_ref, ones_ref, *rest):
    outs, carry_ref = rest[:-1], rest[-1]
    t = pl.program_id(1)

    @pl.when(t == 0)
    def _():
        carry_ref[...] = shift_ref[0]

    x = x_ref[0]
    tt = x.shape[0]
    row = lax.broadcasted_iota(jnp.int32, x.shape, 0)
    prev = jnp.where(row == 0, carry_ref[...], pltpu.roll(x, 1, 0))
    carry_ref[...] = x[tt - 1:tt, :]
    vals = _rwkv_prep_math(x, prev, mu_ref[...], w0_ref[...], w2p_ref[...], a0_ref[...],
                           a2p_ref[...], g2_ref[...], kk_ref[...], ka_ref[...], rk_ref[...],
                           ones_ref[...])
    for o_ref, val in zip(outs, vals):
        o_ref[0] = val


def _rwkv_prep_tok_kernel(x_ref, prev_ref, mu_ref, w0_ref, w2p_ref, a0_ref, a2p_ref, g2_ref,
                          kk_ref, ka_ref, rk_ref, ones_ref, *outs):
    vals = _rwkv_prep_math(x_ref[...], prev_ref[...], mu_ref[...], w0_ref[...], w2p_ref[...],
                           a0_ref[...], a2p_ref[...], g2_ref[...], kk_ref[...], ka_ref[...],
                           rk_ref[...], ones_ref[...])
    for o_ref, val in zip(outs, vals):
        o_ref[...] = val


def _rwkv_weights(p, rd):
    zeros_w = jnp.zeros((RW_A_LORA, rd), F32)
    zeros_a = jnp.zeros((RW_W_LORA, rd), F32)
    return [p['rw_mu'][None, :], p['rw_w0'][None, :],
            jnp.concatenate([p['rw_w2'], zeros_w], axis=0).astype(BF16), p['rw_a0'][None, :],
            jnp.concatenate([zeros_a, p['rw_a2']], axis=0).astype(BF16), p['rw_g2'].astype(BF16),
            p['rw_k_k'][None, :], p['rw_k_a'][None, :], p['rw_r_k'].reshape(1, rd)]


def _weight_specs(weights, nargs):
    zero = {1: lambda i: (0, 0), 2: lambda i, j: (0, 0)}[nargs]
    return [pl.BlockSpec(w.shape, zero) for w in weights]


def _rwkv_prep_seq(prw, shift, weights, ones_bd, *, tt):
    b, s, c = prw.shape
    rd = ones_bd.shape[0]
    consts = weights + [ones_bd]
    out_spec = pl.BlockSpec((1, tt, rd), lambda bi, t: (bi, t, 0))
    return pl.pallas_call(
        _rwkv_prep_seq_kernel,
        grid=(b, s // tt),
        in_specs=[pl.BlockSpec((1, tt, c), lambda bi, t: (bi, t, 0)),
                  pl.BlockSpec((1, 1, c), lambda bi, t: (bi, 0, 0))] + _weight_specs(consts, 2),
        out_specs=[out_spec] * 8,
        out_shape=[jax.ShapeDtypeStruct((b, s, rd), F32)] * 8,
        scratch_shapes=[pltpu.VMEM((1, c), F32)],
        compiler_params=_cparams(("parallel", "arbitrary")),
        name="rwkv_prep_seq",
    )(prw, shift, *consts)


def _rwkv_prep_tok(prw, prev, weights, ones_bd):
    m, c = prw.shape
    rd = ones_bd.shape[0]
    consts = weights + [ones_bd]
    return pl.pallas_call(
        _rwkv_prep_tok_kernel,
        grid=(1,),
        in_specs=[pl.BlockSpec((m, c), lambda i: (0, 0))] * 2 + _weight_specs(consts, 1),
        out_specs=[pl.BlockSpec((m, rd), lambda i: (0, 0))] * 8,
        out_shape=[jax.ShapeDtypeStruct((m, rd), F32)] * 8,
        compiler_params=_cparams(("arbitrary",)),
        name="rwkv_prep_tok",
    )(prw, prev, *consts)


def _rwkv_masks():
    n = RW_NG
    row = np.arange(n)[:, None]
    col = np.arange(n)[None, :]
    same = (row // RW_CHUNK) == (col // RW_CHUNK)
    masks = [same & (col < row), same & (col <= row), (row // 2 == col // 2) & (col < row)]
    s = 2
    while s < RW_CHUNK:
        masks.append(((row // s) % 2 == 1) & ((col // s) == (row // s) - 1))
        s *= 2
    return np.stack(masks).astype(np.float32)


def _rwkv_chunk_chains(chains, masks_ref, tri, s_olds):
    c, ng = chains[0][0].shape
    lane = lax.broadcasted_iota(jnp.int32, (c, ng), 1)
    row = lax.broadcasted_iota(jnp.int32, (ng, ng), 0)
    col = lax.broadcasted_iota(jnp.int32, (ng, ng), 1)
    d = functools.partial(jnp.dot, preferred_element_type=F32)
    cat0 = lambda *xs: jnp.concatenate(xs, axis=0)
    cat1 = lambda *xs: jnp.concatenate(xs, axis=1)

    def stack(x):
        return cat0(*[jnp.where(lane // RW_HEAD == h, x, 0.0) for h in range(ng // RW_HEAD)])

    def split3(logw):
        hi = logw.astype(BF16)
        rem = logw - hi.astype(F32)
        mid = rem.astype(BF16)
        return hi, mid, (rem - mid.astype(F32)).astype(BF16)

    parts = [split3(ch[5]) for ch in chains]
    cums = [d(tri, hi) + d(tri, mid) + d(tri, lo) for hi, mid, lo in parts]

    def decayed(ch, cum):
        r, k2, v, na, b, logw = ch
        cum_c = cum[c - 1:c, :]
        g_inv = jnp.exp(-cum)
        g_tail = jnp.exp(cum_c - cum)
        return dict(at=stack(na * jnp.exp(cum - logw)), rt=stack(r * jnp.exp(cum)),
                    btkt=cat0(stack(b * g_inv), stack(k2 * g_inv)),
                    bhkh=cat0(stack(b * g_tail), stack(k2 * g_tail)), vm=stack(v),
                    gc=jnp.exp(cum_c))

    q = [decayed(ch, cum) for ch, cum in zip(chains, cums)]
    grams = [_dot_nt(cat0(x['at'], x['rt']), x['btkt']) for x in q]
    strict = masks_ref[0]
    incl = masks_ref[1]
    ns = [g[:ng, :ng] * strict for g in grams]
    gaks = [g[:ng, ng:] * strict for g in grams]
    grbs = [g[ng:, :ng] * incl for g in grams]
    grks = [g[ng:, ng:] * incl for g in grams]

    eye = jnp.where(row == col, 1.0, 0.0)
    xs = [eye + n * masks_ref[2] for n in ns]
    for lvl in range(3, masks_ref.shape[0]):
        lvl_mask = masks_ref[lvl]
        ys = [_dot(x, n * lvl_mask) for x, n in zip(xs, ns)]
        xs = [x + _dot(y, x) for x, y in zip(xs, ys)]

    gvs = [_dot(gak, x['vm']) for gak, x in zip(gaks, q)]
    aus = [_dot(x, cat1(y['at'], gv)) for x, y, gv in zip(xs, q, gvs)]
    ts = [_dot(grb, au) for grb, au in zip(grbs, aus)]
    gkvs = [_dot(grk, x['vm']) for grk, x in zip(grks, q)]
    rbs = [x['rt'] + t[:, :ng] for x, t in zip(q, ts)]
    obs = [t[:, ng:] + gkv for t, gkv in zip(ts, gkvs)]
    ms = [jnp.where(row == col, x['gc'], 0.0) + _dot(au[:, :ng].T, x['bhkh'][:ng])
          for x, au in zip(q, aus)]
    zs = [_dot(cat0(au[:, ng:], x['vm']).T, x['bhkh']) for x, au in zip(q, aus)]

    os_ = [_dot_nt(rb, s_old) + ob for rb, s_old, ob in zip(rbs, s_olds, obs)]
    s_news = [_dot(s_old, m) + z for s_old, m, z in zip(s_olds, ms, zs)]
    o_toks = [functools.reduce(jnp.add, [o[h * c:(h + 1) * c] for h in range(ng // RW_HEAD)])
              for o in os_]
    return o_toks, s_news


def _rwkv_chunk_kernel(r_ref, k_ref, v_ref, na_ref, b_ref, lw_ref, masks_ref, tri_ref,
                       o_ref, sout_ref, s_ref, *, bb, n_groups, n_chunks):
    ci = pl.program_id(1)

    @pl.when(ci == 0)
    def _():
        s_ref[...] = jnp.zeros_like(s_ref)

    ids = [(bi, gi) for bi in range(bb) for gi in range(n_groups)]
    sl = lambda gi: slice(gi * RW_NG, (gi + 1) * RW_NG)
    chains = [tuple(ref[bi, :, sl(gi)] for ref in (r_ref, k_ref, v_ref, na_ref, b_ref, lw_ref))
              for bi, gi in ids]
    o_toks, s_news = _rwkv_chunk_chains(chains, masks_ref, tri_ref[...],
                                        [s_ref[bi * n_groups + gi] for bi, gi in ids])
    for (bi, gi), o_tok, s_new in zip(ids, o_toks, s_news):
        o_ref[bi, :, sl(gi)] = o_tok
        s_ref[bi * n_groups + gi] = s_new

    @pl.when(ci == n_chunks - 1)
    def _():
        for bi, gi in ids:
            s = s_ref[bi * n_groups + gi]
            for h in range(RW_GROUP):
                hs = slice(h * RW_HEAD, (h + 1) * RW_HEAD)
                sout_ref[bi, gi * RW_GROUP + h] = s[hs, hs]


def _rwkv_chunked(r, k2, v, na, b, logw, *, bb):
    bsz, s, rd = r.shape
    n_groups = rd // RW_NG
    n_chunks = s // RW_CHUNK
    n_heads = rd // RW_HEAD
    masks = jnp.asarray(_rwkv_masks())
    tri = jnp.asarray(np.tril(np.ones((RW_CHUNK, RW_CHUNK), np.float32))).astype(BF16)
    tok = pl.BlockSpec((bb, RW_CHUNK, rd), lambda bi, ci: (bi, ci, 0))
    return pl.pallas_call(
        functools.partial(_rwkv_chunk_kernel, bb=bb, n_groups=n_groups, n_chunks=n_chunks),
        grid=(bsz // bb, n_chunks),
        in_specs=[tok] * 6 + [pl.BlockSpec(masks.shape, lambda bi, ci: (0, 0, 0)),
                              pl.BlockSpec(tri.shape, lambda bi, ci: (0, 0))],
        out_specs=[tok, pl.BlockSpec((bb, n_heads, RW_HEAD, RW_HEAD), lambda bi, ci: (bi, 0, 0, 0))],
        out_shape=[jax.ShapeDtypeStruct((bsz, s, rd), F32),
                   jax.ShapeDtypeStruct((bsz, n_heads, RW_HEAD, RW_HEAD), F32)],
        scratch_shapes=[pltpu.VMEM((bb * n_groups, RW_NG, RW_NG), F32)],
        compiler_params=_cparams(("parallel", "arbitrary")),
        name="rwkv_chunked",
    )(r, k2, v, na, b, logw, masks, tri)


def _rwkv_step_kernel(s_ref, r_ref, k_ref, na_ref, b_ref, lw_ref, v_ref, sout_ref, o_ref):
    s = s_ref[0]
    sa = jnp.sum(s * na_ref[0], axis=-1, keepdims=True)
    s_new = s * jnp.exp(lw_ref[0]) + sa * b_ref[0] + v_ref[0] * k_ref[0]
    sout_ref[0] = s_new
    o_ref[0] = jnp.sum(s_new * r_ref[0], axis=-1, keepdims=True)


def _rwkv_step(state, r, k2, v, na, b, logw):
    db, nh, hd, _ = state.shape
    rowv = lambda x: x.reshape(db, nh, 1, hd)
    rspec = pl.BlockSpec((1, nh, 1, hd), lambda i: (i, 0, 0, 0))
    cspec = pl.BlockSpec((1, nh, hd, 1), lambda i: (i, 0, 0, 0))
    sspec = pl.BlockSpec((1, nh, hd, hd), lambda i: (i, 0, 0, 0))
    s_new, o = pl.pallas_call(
        _rwkv_step_kernel,
        grid=(db,),
        in_specs=[sspec, rspec, rspec, rspec, rspec, rspec, cspec],
        out_specs=[sspec, cspec],
        out_shape=[jax.ShapeDtypeStruct(state.shape, F32), jax.ShapeDtypeStruct((db, nh, hd, 1), F32)],
        compiler_params=_cparams(("parallel",)),
        name="rwkv_step",
    )(state, rowv(r), rowv(k2), rowv(na), rowv(b), rowv(logw), v.reshape(db, nh, hd, 1))
    return o.reshape(db, nh * hd), s_new


def _outproj_ln_kernel(oda_ref, orw_ref, bonus_ref, gate_ref, h_ref, w_ref, lg_ref, lb_ref,
                       ones_ref, g_ref, b_ref, o_ref, *, alpha):
    o = orw_ref[...]
    ones_bd = ones_ref[...]
    inv_n = 1.0 / RW_HEAD
    mu = _dot_exact_rhs(o, ones_bd) * inv_n
    d = o - mu
    var = _dot_exact_rhs(d * d, ones_bd) * inv_n
    o = d * lax.rsqrt(var + RW_GN_EPS) * lg_ref[...] + lb_ref[...]
    o_rw = (o + bonus_ref[...]) * gate_ref[...]
    da = oda_ref.shape[1]
    mix = (jnp.dot(oda_ref[...], w_ref[0:da, :], preferred_element_type=F32)
           + jnp.dot(o_rw.astype(BF16), w_ref[da:, :], preferred_element_type=F32))
    o_ref[...] = _layernorm(alpha * h_ref[...] + mix, g_ref[...], b_ref[...])


def _outproj_ln(oda, orw, bonus, gate, h, w_out, lnx_g, lnx_b, ones_bd, g, b, *, alpha, tm):
    m, d = h.shape
    da = oda.shape[1]
    rd = orw.shape[1]
    tok = lambda n: pl.BlockSpec((tm, n), lambda i: (i, 0))
    consts = [w_out, lnx_g, lnx_b, ones_bd, g, b]
    return pl.pallas_call(
        functools.partial(_outproj_ln_kernel, alpha=alpha),
        grid=(m // tm,),
        in_specs=[tok(da), tok(rd), tok(rd), tok(rd), tok(d)] + _weight_specs(consts, 1),
        out_specs=tok(d),
        out_shape=jax.ShapeDtypeStruct((m, d), F32),
        compiler_params=_cparams(("parallel",)),
        name="outproj_ln",
    )(oda, orw, bonus, gate, h, *consts)


def _tile(m, pref):
    return pref if m % pref == 0 else m


def kernel(x_prompt, x_sample, cache_k, cache_v, state_wkv, state_shift, page_table, ffa_w_gu, ffa_w_down, ffb_w_gu, ffb_w_down, ln1_g, ln1_b, ln2_g, ln2_b, ln3_g, ln3_b, w_in, w_out, da_lq1, da_lk1, da_lq2, da_lk2, da_subln_g, rw_mu, rw_w0, rw_w2, rw_a0, rw_a2, rw_g2, rw_k_k, rw_k_a, rw_r_k, rw_lnx_g, rw_lnx_b):
    bsz, seq, d_model = x_prompt.shape
    db, dec_seq, _ = x_sample.shape
    assert dec_seq == 1, "the sample group is written for one new token per sequence"
    depth = w_in.shape[0]
    n_pool, page, n_heads, _ = cache_k.shape[1:]
    n_pages = page_table.shape[1]
    rd = rw_w0.shape[1]
    rw_heads = rd // RW_HEAD
    d_ff = ffa_w_down.shape[1]
    alpha = (2 * depth) ** 0.25
    width = n_heads * V7X_LANES

    cos_p, sin_p = _rope_tables(jnp.arange(seq))
    cos_s, sin_s = _rope_tables(jnp.full((db,), n_pages * page, jnp.int32))
    ones_bd = jnp.asarray(np.kron(np.eye(rw_heads, dtype=np.float32),
                                  np.ones((RW_HEAD, RW_HEAD), np.float32))).astype(BF16)

    m_p = bsz * seq
    tm_p = _tile(m_p, 512)
    tf = _tile(d_ff, 1408)
    y_p = x_prompt.reshape(m_p, d_model)
    y_s = x_sample.reshape(db, d_model)
    outs = [[] for _ in range(8)]
    row = lambda a: a[None, :]

    for l in range(depth):
        lam_init = 0.8 - 0.6 * math.exp(-0.3 * l)
        wa_gu, wa_dn = ffa_w_gu[l].astype(BF16), ffa_w_down[l].astype(BF16)
        wb_gu, wb_dn = ffb_w_gu[l].astype(BF16), ffb_w_down[l].astype(BF16)
        wi, wo = w_in[l].astype(BF16), w_out[l].astype(BF16)
        lam_vecs = [row(da_lq1[l]), row(da_lk1[l]), row(da_lq2[l]), row(da_lk2[l]), row(da_subln_g[l])]
        rw_weights = _rwkv_weights(dict(rw_mu=rw_mu[l], rw_w0=rw_w0[l], rw_w2=rw_w2[l], rw_a0=rw_a0[l],
                                        rw_a2=rw_a2[l], rw_g2=rw_g2[l], rw_k_k=rw_k_k[l],
                                        rw_k_a=rw_k_a[l], rw_r_k=rw_r_k[l]), rd)
        ln = [row(a[l]) for a in (ln1_g, ln1_b, ln2_g, ln2_b, ln3_g, ln3_b)]
        lnx = [row(rw_lnx_g[l]), row(rw_lnx_b[l])]

        h1 = _ffn_ln(y_p, wa_gu, wa_dn, ln[0], ln[1], alpha=alpha, tm=tm_p, tf=tf)
        q, k, kb, v, vb, prw = _inproj(h1, wi, cos_p, sin_p, n_heads=n_heads, tm=tm_p)
        oda = _attn_prompt(q.reshape(bsz, seq, width), kb.reshape(bsz, seq, width),
                           vb.reshape(bsz, seq, width), *lam_vecs,
                           n_heads=n_heads, tq=512, rs=128, lam_init=lam_init)
        prw3 = prw.reshape(bsz, seq, -1)
        shift0 = jnp.zeros((bsz, 1, prw3.shape[-1]), F32)
        r_, k2_, v_, na_, b_, lw_, gate, bonus = _rwkv_prep_seq(prw3, shift0, rw_weights, ones_bd, tt=512)
        orw, wkv_p = _rwkv_chunked(r_, k2_, v_, na_, b_, lw_, bb=2)
        h2 = _outproj_ln(oda.reshape(m_p, width), orw.reshape(m_p, rd), bonus.reshape(m_p, rd),
                         gate.reshape(m_p, rd), h1, wo, *lnx, ones_bd, ln[2], ln[3],
                         alpha=alpha, tm=tm_p)
        y_p = _ffn_ln(h2, wb_gu, wb_dn, ln[4], ln[5], alpha=alpha, tm=tm_p, tf=tf)
        outs[0].append(k.reshape(bsz, seq, n_heads, 2 * DA_QK))
        outs[1].append(v.reshape(bsz, seq, n_heads, DA_V))
        outs[2].append(wkv_p)
        outs[3].append(prw3[:, seq - 1:seq, :])

        h1 = _ffn_ln(y_s, wa_gu, wa_dn, ln[0], ln[1], alpha=alpha, tm=db, tf=tf)
        q, k, kb, v, vb, prw = _inproj(h1, wi, cos_s, sin_s, n_heads=n_heads, tm=db)
        oda = _attn_decode(q, kb, vb, cache_k.reshape(depth * n_pool, page, width),
                           cache_v.reshape(depth * n_pool, page, width), page_table, l * n_pool,
                           *lam_vecs, n_heads=n_heads, gp=8, lam_init=lam_init)
        r_, k2_, v_, na_, b_, lw_, gate, bonus = _rwkv_prep_tok(
            prw, state_shift[l].reshape(db, -1), rw_weights, ones_bd)
        orw, wkv_s = _rwkv_step(state_wkv[l], r_, k2_, v_, na_, b_, lw_)
        h2 = _outproj_ln(oda, orw, bonus, gate, h1, wo, *lnx, ones_bd, ln[2], ln[3],
                         alpha=alpha, tm=db)
        y_s = _ffn_ln(h2, wb_gu, wb_dn, ln[4], ln[5], alpha=alpha, tm=db, tf=tf)
        outs[4].append(k.reshape(db, 1, n_heads, 2 * DA_QK))
        outs[5].append(v.reshape(db, 1, n_heads, DA_V))
        outs[6].append(wkv_s)
        outs[7].append(prw.reshape(db, 1, -1))

    stacked = [jnp.stack(o) for o in outs]
    return (y_p.reshape(bsz, seq, d_model), y_s.reshape(db, 1, d_model), *stacked)
```

```python
import functools
import math

import jax
import jax.numpy as jnp
import numpy as np
from jax import lax
from jax.experimental import pallas as pl
from jax.experimental.pallas import tpu as pltpu

F32 = jnp.float32
BF16 = jnp.bfloat16

DA_QK = 64
DA_V = 2 * DA_QK
RW_HEAD = 64
RW_W_LORA = 64
RW_A_LORA = 64
RW_G_LORA = 128
RW_GN_EPS = 64e-5
ROPE_THETA = 10000.0
LN_EPS = 1e-5
NEG = -1e30
LOG2E = math.log2(math.e)

V7X_LANES = 128
V7X_VMEM_LIMIT_BYTES = 56 * 1024 * 1024

RW_CHUNK = 64
RW_GROUP = 4
RW_NG = RW_CHUNK * RW_GROUP


def _cparams(semantics):
    return pltpu.CompilerParams(dimension_semantics=semantics,
                                vmem_limit_bytes=V7X_VMEM_LIMIT_BYTES)


def _dot(a, b):
    return jnp.dot(a.astype(BF16), b.astype(BF16), preferred_element_type=F32)


def _dot_nt(a, b):
    return lax.dot_general(a.astype(BF16), b.astype(BF16), (((1,), (1,)), ((), ())),
                           preferred_element_type=F32)


def _split(x):
    hi = x.astype(BF16)
    lo = (x - hi.astype(F32)).astype(BF16)
    return hi, lo


def _dot_exact_rhs(a, b):
    ah, al = _split(a)
    return (jnp.dot(ah, b, preferred_element_type=F32)
            + jnp.dot(al, b, preferred_element_type=F32))


def _layernorm(y, g, b):
    mu = jnp.mean(y, axis=-1, keepdims=True)
    d = y - mu
    var = jnp.mean(d * d, axis=-1, keepdims=True)
    return d * lax.rsqrt(var + LN_EPS) * g + b


def _ffn_ln_kernel(x_ref, wgu_ref, wd_ref, g_ref, b_ref, o_ref, *, d_ff, tf, alpha):
    x = x_ref[...]
    xb = x.astype(BF16)

    def activation(c):
        gate = jnp.dot(xb, wgu_ref[:, c * tf:(c + 1) * tf], preferred_element_type=F32)
        up = jnp.dot(xb, wgu_ref[:, d_ff + c * tf:d_ff + (c + 1) * tf], preferred_element_type=F32)
        return (gate * jax.nn.sigmoid(gate) * up).astype(BF16)

    nf = d_ff // tf
    act = activation(0)
    acc = None
    for c in range(nf):
        nxt = activation(c + 1) if c + 1 < nf else None
        part = jnp.dot(act, wd_ref[c * tf:(c + 1) * tf, :], preferred_element_type=F32)
        acc = part if acc is None else acc + part
        act = nxt
    o_ref[...] = _layernorm(alpha * x + 0.5 * acc, g_ref[...], b_ref[...])


def _ffn_ln(x, w_gu, w_down, g, b, *, alpha, tm, tf):
    m, d = x.shape
    f = w_down.shape[0]
    assert m % tm == 0 and f % tf == 0
    const = lambda shape: pl.BlockSpec(shape, lambda i: (0, 0))
    return pl.pallas_call(
        functools.partial(_ffn_ln_kernel, d_ff=f, tf=tf, alpha=alpha),
        grid=(m // tm,),
        in_specs=[pl.BlockSpec((tm, d), lambda i: (i, 0)), const(w_gu.shape), const(w_down.shape),
                  const((1, d)), const((1, d))],
        out_specs=pl.BlockSpec((tm, d), lambda i: (i, 0)),
        out_shape=jax.ShapeDtypeStruct((m, d), F32),
        compiler_params=_cparams(("parallel",)),
        name="ffn_ln",
    )(x, w_gu, w_down, g, b)


def _inproj_kernel(h_ref, w_ref, cos_ref, sin_ref,
                   q_ref, k_ref, kb_ref, v_ref, vb_ref, prw_ref, *, n_heads, scale):
    hb = h_ref[...].astype(BF16)
    cos = cos_ref[...]
    sin = sin_ref[...]
    lane = lax.broadcasted_iota(jnp.int32, cos.shape, 1)
    first_half = (lane % DA_QK) < (DA_QK // 2)
    qc = n_heads * 2 * DA_QK

    def rope_head(x):
        rot = jnp.where(first_half, pltpu.roll(x, V7X_LANES - DA_QK // 2, 1),
                        pltpu.roll(x, DA_QK // 2, 1))
        return x * cos + rot * sin

    q = jnp.dot(hb, w_ref[:, 0:qc], preferred_element_type=F32)
    k = jnp.dot(hb, w_ref[:, qc:2 * qc], preferred_element_type=F32)
    for h in range(n_heads):
        sl = slice(h * V7X_LANES, (h + 1) * V7X_LANES)
        qh = rope_head(q[:, sl])
        kh = rope_head(k[:, sl])
        q_ref[:, sl] = (qh * scale).astype(BF16)
        k_ref[:, sl] = kh
        kb_ref[:, sl] = kh.astype(BF16)
    vc = n_heads * DA_V
    v = jnp.dot(hb, w_ref[:, 2 * qc:2 * qc + vc], preferred_element_type=F32)
    v_ref[...] = v
    vb_ref[...] = v.astype(BF16)
    prw_ref[...] = jnp.dot(hb, w_ref[:, 2 * qc + vc:], preferred_element_type=F32)


def _inproj(h, w_in, cos_tab, sin_tab, *, n_heads, tm):
    m, d = h.shape
    cols = w_in.shape[1]
    qc = n_heads * 2 * DA_QK
    vc = n_heads * DA_V
    rw = cols - 2 * qc - vc
    n_pos_blocks = cos_tab.shape[0] // tm
    tok = lambda i: (i, 0)
    pos = lambda i: (i % n_pos_blocks, 0)
    return pl.pallas_call(
        functools.partial(_inproj_kernel, n_heads=n_heads, scale=DA_QK ** -0.5 * LOG2E),
        grid=(m // tm,),
        in_specs=[
            pl.BlockSpec((tm, d), tok),
            pl.BlockSpec((d, cols), lambda i: (0, 0)),
            pl.BlockSpec((tm, V7X_LANES), pos),
            pl.BlockSpec((tm, V7X_LANES), pos),
        ],
        out_specs=[
            pl.BlockSpec((tm, qc), tok), pl.BlockSpec((tm, qc), tok), pl.BlockSpec((tm, qc), tok),
            pl.BlockSpec((tm, vc), tok), pl.BlockSpec((tm, vc), tok), pl.BlockSpec((tm, rw), tok),
        ],
        out_shape=[
            jax.ShapeDtypeStruct((m, qc), BF16), jax.ShapeDtypeStruct((m, qc), F32),
            jax.ShapeDtypeStruct((m, qc), BF16), jax.ShapeDtypeStruct((m, vc), F32),
            jax.ShapeDtypeStruct((m, vc), BF16), jax.ShapeDtypeStruct((m, rw), F32),
        ],
        compiler_params=_cparams(("parallel",)),
        name="inproj_rope",
    )(h, w_in, cos_tab, sin_tab)


def _rope_tables(positions):
    half = DA_QK // 2
    inv = ROPE_THETA ** (-jnp.arange(half, dtype=F32) / half)
    ang = positions.astype(F32)[:, None] * inv[None, :]
    cos = jnp.cos(ang)
    sin = jnp.sin(ang)
    cos_tab = jnp.concatenate([cos, cos, cos, cos], axis=1)
    sin_tab = jnp.concatenate([-sin, sin, -sin, sin], axis=1)
    return cos_tab, sin_tab


def _lambda(lq1_ref, lk1_ref, lq2_ref, lk2_ref, lam_init):
    s1 = jnp.sum(lq1_ref[...] * lk1_ref[...], axis=-1, keepdims=True)
    s2 = jnp.sum(lq2_ref[...] * lk2_ref[...], axis=-1, keepdims=True)
    return jnp.exp(s1) - jnp.exp(s2) + lam_init


def _subln(o, g, lam_init):
    return o * lax.rsqrt(jnp.mean(o * o, axis=-1, keepdims=True) + LN_EPS) * g * (1.0 - lam_init)


def _attn_prompt_kernel(lq1_ref, lk1_ref, lq2_ref, lk2_ref, g_ref, q_ref, k_ref, v_ref, o_ref,
                        qm_ref, s_ref, m_ref, l_ref, acc_ref, *, tq, rs, lam_init):
    i = pl.program_id(2)
    q = q_ref[0].astype(F32)
    lane = lax.broadcasted_iota(jnp.int32, q.shape, 1)
    qm_ref[0] = jnp.where(lane < DA_QK, q, 0.0).astype(BF16)
    qm_ref[1] = jnp.where(lane >= DA_QK, q, 0.0).astype(BF16)
    m_ref[...] = jnp.full(m_ref.shape, NEG, F32)
    l_ref[...] = jnp.zeros(l_ref.shape, F32)
    acc_ref[...] = jnp.zeros(acc_ref.shape, F32)
    nt = tq // V7X_LANES

    def scores(j, buf):
        kblk = k_ref[0, pl.ds(pl.multiple_of(j * tq, tq), tq), :]
        dn = (((1,), (1,)), ((), ()))
        for mp in range(2):
            s_ref[buf, mp] = lax.dot_general(qm_ref[mp], kblk, dn, preferred_element_type=F32)

    def softmax_pv(j, buf, diagonal):
        off = pl.multiple_of(j * tq, tq)
        for r in range(tq // rs):
            rows = slice(r * rs, (r + 1) * rs)
            n_vis = min(nt, pl.cdiv((r + 1) * rs, V7X_LANES)) if diagonal else nt
            vblk = v_ref[0, pl.ds(off, n_vis * V7X_LANES), :]
            for mp in range(2):
                tiles = []
                for t in range(n_vis):
                    s = s_ref[buf, mp, rows, t * V7X_LANES:(t + 1) * V7X_LANES]
                    if diagonal and (t + 1) * V7X_LANES > r * rs + 1:
                        row = lax.broadcasted_iota(jnp.int32, s.shape, 0) + r * rs
                        col = lax.broadcasted_iota(jnp.int32, s.shape, 1) + t * V7X_LANES
                        s = jnp.where(col <= row, s, NEG)
                    tiles.append(s)
                mx = functools.reduce(jnp.maximum, tiles)
                m_old = m_ref[mp, rows, :]
                m_new = jnp.maximum(m_old, jnp.max(mx, axis=1, keepdims=True))
                ps = [jnp.exp2(s - m_new) for s in tiles]
                c = jnp.exp2(m_old - m_new)
                rowsum = jnp.sum(functools.reduce(jnp.add, ps), axis=1, keepdims=True)
                l_ref[mp, rows, :] = c * l_ref[mp, rows, :] + rowsum
                m_ref[mp, rows, :] = m_new
                p = jnp.concatenate([x.astype(BF16) for x in ps], axis=1)
                pv = jnp.dot(p, vblk, preferred_element_type=F32)
                acc_ref[mp, rows, :] = c * acc_ref[mp, rows, :] + pv

    scores(0, 0)

    def body(jj, carry):
        j = 2 * jj
        scores(j + 1, 1)
        softmax_pv(j, 0, False)
        scores(j + 2, 0)
        softmax_pv(j + 1, 1, False)
        return carry

    lax.fori_loop(0, i // 2, body, 0)

    @pl.when(i % 2 == 0)
    def _():
        softmax_pv(i, 0, True)

    @pl.when(i % 2 == 1)
    def _():
        scores(i, 1)
        softmax_pv(i - 1, 0, False)
        softmax_pv(i, 1, True)

    lam = _lambda(lq1_ref, lk1_ref, lq2_ref, lk2_ref, lam_init)
    o = acc_ref[0] / l_ref[0] - lam * (acc_ref[1] / l_ref[1])
    o_ref[0] = _subln(o, g_ref[...], lam_init).astype(o_ref.dtype)


def _attn_prompt(q, kb, vb, lq1, lk1, lq2, lk2, subln_g, *, n_heads, tq, rs, lam_init):
    b, s, _ = q.shape
    vec = lambda n: pl.BlockSpec((1, n), lambda bi, h, i: (0, 0))
    return pl.pallas_call(
        functools.partial(_attn_prompt_kernel, tq=tq, rs=rs, lam_init=lam_init),
        grid=(b, n_heads, s // tq),
        in_specs=[
            vec(DA_QK), vec(DA_QK), vec(DA_QK), vec(DA_QK), vec(DA_V),
            pl.BlockSpec((1, tq, V7X_LANES), lambda bi, h, i: (bi, i, h)),
            pl.BlockSpec((1, s, V7X_LANES), lambda bi, h, i: (bi, 0, h)),
            pl.BlockSpec((1, s, DA_V), lambda bi, h, i: (bi, 0, h)),
        ],
        out_specs=pl.BlockSpec((1, tq, DA_V), lambda bi, h, i: (bi, i, h)),
        out_shape=jax.ShapeDtypeStruct((b, s, n_heads * DA_V), BF16),
        scratch_shapes=[pltpu.VMEM((2, tq, V7X_LANES), BF16), pltpu.VMEM((2, 2, tq, tq), F32),
                        pltpu.VMEM((2, tq, DA_V), F32), pltpu.VMEM((2, tq, DA_V), F32),
                        pltpu.VMEM((2, tq, DA_V), F32)],
        compiler_params=_cparams(("parallel", "parallel", "arbitrary")),
        name="attn_prompt",
    )(lq1, lk1, lq2, lk2, subln_g, q, kb, vb)


def _attn_decode_kernel(pt_ref, lq1_ref, lk1_ref, lq2_ref, lk2_ref, g_ref, q_ref, kn_ref, vn_ref,
                        *rest, n_heads, n_steps, gp, lam_init):
    del pt_ref
    kp_refs, vp_refs = rest[:gp], rest[gp:2 * gp]
    o_ref, qm_ref, ex_ref, m_ref, l_ref, acc_ref = rest[2 * gp:]
    p = pl.program_id(1)
    nj = 2 * n_heads
    dn = (((1,), (1,)), ((), ()))
    sub = lax.broadcasted_iota(jnp.int32, (8, V7X_LANES), 0)
    lane = lax.broadcasted_iota(jnp.int32, (8, V7X_LANES), 1)
    live = (sub % n_heads) == (lane // 2)

    def head_rows(x):
        out = jnp.zeros((8, V7X_LANES), F32)
        for h in range(n_heads):
            xh = jnp.broadcast_to(x[:, h * V7X_LANES:(h + 1) * V7X_LANES], (8, V7X_LANES))
            out = jnp.where(sub == h, xh, out)
        return out

    def per_row(x, mp):
        pick = lane == 2 * (sub % n_heads) + mp
        return jnp.broadcast_to(jnp.sum(jnp.where(pick, x, 0.0), axis=1, keepdims=True),
                                (8, V7X_LANES))

    def softmax_rows(s, rows_live):
        r = s.shape[0]
        s3 = jnp.where(rows_live, s.reshape(r // 8, 8, V7X_LANES), NEG)
        m_g = jnp.max(jnp.max(s3, axis=0), axis=0, keepdims=True)
        pr = jnp.exp2(s3 - m_g).reshape(r, V7X_LANES)
        return m_g, pr, jnp.sum(pr, axis=0, keepdims=True)

    def weighted_values(pb, v):
        r = v.shape[0]
        return [jnp.sum((pb[:, mp * V7X_LANES:(mp + 1) * V7X_LANES] * v)
                        .reshape(r // 8, 8, V7X_LANES), axis=0) for mp in range(2)]

    @pl.when(p == 0)
    def _():
        e_r = lax.broadcasted_iota(jnp.int32, ex_ref.shape, 0)
        e_c = lax.broadcasted_iota(jnp.int32, ex_ref.shape, 1)
        ex_ref[...] = jnp.where((e_r < nj) & (e_r % 2 == e_c // V7X_LANES), 1.0, 0.0).astype(BF16)
        qrows = jnp.zeros((8, V7X_LANES), F32)
        q = q_ref[0].astype(F32)
        for h in range(n_heads):
            qh = jnp.broadcast_to(q[:, h * V7X_LANES:(h + 1) * V7X_LANES], (8, V7X_LANES))
            qrows = jnp.where(sub // 2 == h, qh, qrows)
        qrows = jnp.where(lane // DA_QK == sub % 2, qrows, 0.0)
        qm_ref[...] = jnp.zeros(qm_ref.shape, BF16)
        qm_ref[0:8, :] = qrows.astype(BF16)
        kn = head_rows(kn_ref[0].astype(F32))
        s_new = lax.dot_general(kn.astype(BF16), qm_ref[...], dn, preferred_element_type=F32)
        m_g, pr, l_g = softmax_rows(s_new, live & (sub < n_heads))
        m_ref[...] = m_g
        l_ref[...] = l_g
        pb = jnp.dot(pr.astype(BF16), ex_ref[...], preferred_element_type=F32)
        parts = weighted_values(pb, head_rows(vn_ref[0].astype(F32)))
        for mp in range(2):
            acc_ref[mp] = parts[mp]

    qm = qm_ref[...]
    s_pg = [lax.dot_general(kp_refs[g][0].astype(BF16), qm, dn, preferred_element_type=F32)
            for g in range(gp)]
    stats = [softmax_rows(s, live) for s in s_pg]
    ex = ex_ref[...]
    pb_pg = [jnp.dot(st[1].astype(BF16), ex, preferred_element_type=F32) for st in stats]
    part_pg = [weighted_values(pb, vp_refs[g][0]) for g, pb in enumerate(pb_pg)]
    m_old = m_ref[...]
    m_new = functools.reduce(jnp.maximum, [st[0] for st in stats], m_old)
    c = jnp.exp2(m_old - m_new)
    w_pg = [jnp.exp2(st[0] - m_new) for st in stats]
    l_ref[...] = c * l_ref[...] + functools.reduce(
        jnp.add, [w * st[2] for w, st in zip(w_pg, stats)])
    m_ref[...] = m_new
    for mp in range(2):
        acc = per_row(c, mp) * acc_ref[mp]
        for g in range(gp):
            acc = acc + per_row(w_pg[g], mp) * part_pg[g][mp]
        acc_ref[mp] = acc

    @pl.when(p == n_steps - 1)
    def _():
        lam = _lambda(lq1_ref, lk1_ref, lq2_ref, lk2_ref, lam_init)
        on = [acc_ref[mp] / per_row(l_ref[...], mp) for mp in range(2)]
        for h in range(n_heads):
            o1, o2 = [functools.reduce(jnp.add, [x[i:i + 1, :] for i in range(h, 8, n_heads)])
                      for x in on]
            o_ref[0, :, h * V7X_LANES:(h + 1) * V7X_LANES] = _subln(
                o1 - lam * o2, g_ref[...], lam_init).astype(o_ref.dtype)


def _attn_decode(q, kb, vb, cache_k, cache_v, page_table, page_base, lq1, lk1, lq2, lk2, subln_g,
                 *, n_heads, gp, lam_init):
    db, width = q.shape
    n_pages = page_table.shape[1]
    rows = cache_k.shape[1]
    assert 8 % n_heads == 0 and rows % 8 == 0
    n_steps = n_pages // gp
    vec = lambda n: pl.BlockSpec((1, n), lambda bi, p, pt: (0, 0))
    tok = pl.BlockSpec((1, 1, width), lambda bi, p, pt: (bi, 0, 0))
    pages = [pl.BlockSpec((1, rows, V7X_LANES),
                          lambda bi, p, pt, g=g: (page_base + pt[bi, p * gp + g], 0, 0))
             for g in range(gp)]
    grid_spec = pltpu.PrefetchScalarGridSpec(
        num_scalar_prefetch=1,
        grid=(db, n_steps),
        in_specs=[vec(DA_QK), vec(DA_QK), vec(DA_QK), vec(DA_QK), vec(DA_V), tok, tok, tok]
                 + pages + pages,
        out_specs=tok,
        scratch_shapes=[pltpu.VMEM((V7X_LANES, V7X_LANES), BF16),
                        pltpu.VMEM((V7X_LANES, 2 * V7X_LANES), BF16),
                        pltpu.VMEM((1, V7X_LANES), F32), pltpu.VMEM((1, V7X_LANES), F32),
                        pltpu.VMEM((2, 8, V7X_LANES), F32)],
    )
    out = pl.pallas_call(
        functools.partial(_attn_decode_kernel, n_heads=n_heads, n_steps=n_steps, gp=gp,
                          lam_init=lam_init),
        grid_spec=grid_spec,
        out_shape=jax.ShapeDtypeStruct((db, 1, width), BF16),
        compiler_params=_cparams(("parallel", "arbitrary")),
        name="attn_decode",
    )(page_table, lq1, lk1, lq2, lk2, subln_g,
      q.reshape(db, 1, width), kb.reshape(db, 1, width), vb.reshape(db, 1, width),
      *([cache_k] * gp), *([cache_v] * gp))
    return out.reshape(db, width)


def _rwkv_prep_math(x, prev, mu, w0, w2p, a0, a2p, g2, k_k, k_a, r_k, ones_bd):
    rd = ones_bd.shape[0]
    xm = x + (prev - x) * mu
    r = xm[:, 0:rd]
    k = xm[:, rd:2 * rd]
    v = xm[:, 2 * rd:3 * rd]
    xwa = xm[:, 3 * rd:3 * rd + RW_W_LORA + RW_A_LORA]
    xg = xm[:, 3 * rd + RW_W_LORA + RW_A_LORA:]
    zw = -(w0 + _dot(jnp.tanh(xwa), w2p))
    softplus = jnp.maximum(zw, 0.0) + jnp.log(1.0 + jnp.exp(-jnp.abs(zw)))
    logw = -jnp.exp(-softplus - 0.5)
    a = jax.nn.sigmoid(a0 + _dot(xwa, a2p))
    g = _dot(jax.nn.sigmoid(xg), g2)
    kk = k * k_k
    n2 = _dot_exact_rhs(kk * kk, ones_bd)
    kk = kk / jnp.maximum(jnp.sqrt(n2), 1e-12)
    k2 = k * (1.0 + (a - 1.0) * k_a)
    bonus = _dot_exact_rhs(r * k2 * r_k, ones_bd) * v
    return r, k2, v, -kk, kk * a, logw, g, bonus


def _rwkv_prep_seq_kernel(x_ref, shift_ref, mu_ref, w0_ref, w2p_ref, a0_ref, a2p_ref, g2_ref,
                          kk_ref, ka_ref, rk_ref, ones_ref, *rest):
    outs, carry_ref = rest[:-1], rest[-1]
    t = pl.program_id(1)

    @pl.when(t == 0)
    def _():
        carry_ref[...] = shift_ref[0]

    x = x_ref[0]
    tt = x.shape[0]
    row = lax.broadcasted_iota(jnp.int32, x.shape, 0)
    prev = jnp.where(row == 0, carry_ref[...], pltpu.roll(x, 1, 0))
    carry_ref[...] = x[tt - 1:tt, :]
    vals = _rwkv_prep_math(x, prev, mu_ref[...], w0_ref[...], w2p_ref[...], a0_ref[...],
                           a2p_ref[...], g2_ref[...], kk_ref[...], ka_ref[...], rk_ref[...],
                           ones_ref[...])
    for o_ref, val in zip(outs, vals):
        o_ref[0] = val


def _rwkv_prep_tok_kernel(x_ref, prev_ref, mu_ref, w0_ref, w2p_ref, a0_ref, a2p_ref, g2_ref,
                          kk_ref, ka_ref, rk_ref, ones_ref, *outs):
    vals = _rwkv_prep_math(x_ref[...], prev_ref[...], mu_ref[...], w0_ref[...], w2p_ref[...],
                           a0_ref[...], a2p_ref[...], g2_ref[...], kk_ref[...], ka_ref[...],
                           rk_ref[...], ones_ref[...])
    for o_ref, val in zip(outs, vals):
        o_ref[...] = val


def _rwkv_weights(p, rd):
    zeros_w = jnp.zeros((RW_A_LORA, rd), F32)
    zeros_a = jnp.zeros((RW_W_LORA, rd), F32)
    return [p['rw_mu'][None, :], p['rw_w0'][None, :],
            jnp.concatenate([p['rw_w2'], zeros_w], axis=0).astype(BF16), p['rw_a0'][None, :],
            jnp.concatenate([zeros_a, p['rw_a2']], axis=0).astype(BF16), p['rw_g2'].astype(BF16),
            p['rw_k_k'][None, :], p['rw_k_a'][None, :], p['rw_r_k'].reshape(1, rd)]


def _weight_specs(weights, nargs):
    zero = {1: lambda i: (0, 0), 2: lambda i, j: (0, 0)}[nargs]
    return [pl.BlockSpec(w.shape, zero) for w in weights]


def _rwkv_prep_seq(prw, shift, weights, ones_bd, *, tt):
    b, s, c = prw.shape
    rd = ones_bd.shape[0]
    consts = weights + [ones_bd]
    out_spec = pl.BlockSpec((1, tt, rd), lambda bi, t: (bi, t, 0))
    return pl.pallas_call(
        _rwkv_prep_seq_kernel,
        grid=(b, s // tt),
        in_specs=[pl.BlockSpec((1, tt, c), lambda bi, t: (bi, t, 0)),
                  pl.BlockSpec((1, 1, c), lambda bi, t: (bi, 0, 0))] + _weight_specs(consts, 2),
        out_specs=[out_spec] * 8,
        out_shape=[jax.ShapeDtypeStruct((b, s, rd), F32)] * 8,
        scratch_shapes=[pltpu.VMEM((1, c), F32)],
        compiler_params=_cparams(("parallel", "arbitrary")),
        name="rwkv_prep_seq",
    )(prw, shift, *consts)


def _rwkv_prep_tok(prw, prev, weights, ones_bd):
    m, c = prw.shape
    rd = ones_bd.shape[0]
    consts = weights + [ones_bd]
    return pl.pallas_call(
        _rwkv_prep_tok_kernel,
        grid=(1,),
        in_specs=[pl.BlockSpec((m, c), lambda i: (0, 0))] * 2 + _weight_specs(consts, 1),
        out_specs=[pl.BlockSpec((m, rd), lambda i: (0, 0))] * 8,
        out_shape=[jax.ShapeDtypeStruct((m, rd), F32)] * 8,
        compiler_params=_cparams(("arbitrary",)),
        name="rwkv_prep_tok",
    )(prw, prev, *consts)


def _rwkv_masks():
    n = RW_NG
    row = np.arange(n)[:, None]
    col = np.arange(n)[None, :]
    same = (row // RW_CHUNK) == (col // RW_CHUNK)
    masks = [same & (col < row), same & (col <= row), (row // 2 == col // 2) & (col < row)]
    s = 2
    while s < RW_CHUNK:
        masks.append(((row // s) % 2 == 1) & ((col // s) == (row // s) - 1))
        s *= 2
    return np.stack(masks).astype(np.float32)


def _rwkv_chunk_chains(chains, masks_ref, tri, s_olds):
    c, ng = chains[0][0].shape
    lane = lax.broadcasted_iota(jnp.int32, (c, ng), 1)
    row = lax.broadcasted_iota(jnp.int32, (ng, ng), 0)
    col = lax.broadcasted_iota(jnp.int32, (ng, ng), 1)
    d = functools.partial(jnp.dot, preferred_element_type=F32)
    cat0 = lambda *xs: jnp.concatenate(xs, axis=0)
    cat1 = lambda *xs: jnp.concatenate(xs, axis=1)

    def stack(x):
        return cat0(*[jnp.where(lane // RW_HEAD == h, x, 0.0) for h in range(ng // RW_HEAD)])

    def split3(logw):
        hi = logw.astype(BF16)
        rem = logw - hi.astype(F32)
        mid = rem.astype(BF16)
        return hi, mid, (rem - mid.astype(F32)).astype(BF16)

    parts = [split3(ch[5]) for ch in chains]
    cums = [d(tri, hi) + d(tri, mid) + d(tri, lo) for hi, mid, lo in parts]

    def decayed(ch, cum):
        r, k2, v, na, b, logw = ch
        cum_c = cum[c - 1:c, :]
        g_inv = jnp.exp(-cum)
        g_tail = jnp.exp(cum_c - cum)
        return dict(at=stack(na * jnp.exp(cum - logw)), rt=stack(r * jnp.exp(cum)),
                    btkt=cat0(stack(b * g_inv), stack(k2 * g_inv)),
                    bhkh=cat0(stack(b * g_tail), stack(k2 * g_tail)), vm=stack(v),
                    gc=jnp.exp(cum_c))

    q = [decayed(ch, cum) for ch, cum in zip(chains, cums)]
    grams = [_dot_nt(cat0(x['at'], x['rt']), x['btkt']) for x in q]
    strict = masks_ref[0]
    incl = masks_ref[1]
    ns = [g[:ng, :ng] * strict for g in grams]
    gaks = [g[:ng, ng:] * strict for g in grams]
    grbs = [g[ng:, :ng] * incl for g in grams]
    grks = [g[ng:, ng:] * incl for g in grams]

    eye = jnp.where(row == col, 1.0, 0.0)
    xs = [eye + n * masks_ref[2] for n in ns]
    for lvl in range(3, masks_ref.shape[0]):
        lvl_mask = masks_ref[lvl]
        ys = [_dot(x, n * lvl_mask) for x, n in zip(xs, ns)]
        xs = [x + _dot(y, x) for x, y in zip(xs, ys)]

    gvs = [_dot(gak, x['vm']) for gak, x in zip(gaks, q)]
    aus = [_dot(x, cat1(y['at'], gv)) for x, y, gv in zip(xs, q, gvs)]
    ts = [_dot(grb, au) for grb, au in zip(grbs, aus)]
    gkvs = [_dot(grk, x['vm']) for grk, x in zip(grks, q)]
    rbs = [x['rt'] + t[:, :ng] for x, t in zip(q, ts)]
    obs = [t[:, ng:] + gkv for t, gkv in zip(ts, gkvs)]
    ms = [jnp.where(row == col, x['gc'], 0.0) + _dot(au[:, :ng].T, x['bhkh'][:ng])
          for x, au in zip(q, aus)]
    zs = [_dot(cat0(au[:, ng:], x['vm']).T, x['bhkh']) for x, au in zip(q, aus)]

    os_ = [_dot_nt(rb, s_old) + ob for rb, s_old, ob in zip(rbs, s_olds, obs)]
    s_news = [_dot(s_old, m) + z for s_old, m, z in zip(s_olds, ms, zs)]
    o_toks = [functools.reduce(jnp.add, [o[h * c:(h + 1) * c] for h in range(ng // RW_HEAD)])
              for o in os_]
    return o_toks, s_news


def _rwkv_chunk_kernel(r_ref, k_ref, v_ref, na_ref, b_ref, lw_ref, masks_ref, tri_ref,
                       o_ref, sout_ref, s_ref, *, bb, n_groups, n_chunks):
    ci = pl.program_id(1)

    @pl.when(ci == 0)
    def _():
        s_ref[...] = jnp.zeros_like(s_ref)

    ids = [(bi, gi) for bi in range(bb) for gi in range(n_groups)]
    sl = lambda gi: slice(gi * RW_NG, (gi + 1) * RW_NG)
    chains = [tuple(ref[bi, :, sl(gi)] for ref in (r_ref, k_ref, v_ref, na_ref, b_ref, lw_ref))
              for bi, gi in ids]
    o_toks, s_news = _rwkv_chunk_chains(chains, masks_ref, tri_ref[...],
                                        [s_ref[bi * n_groups + gi] for bi, gi in ids])
    for (bi, gi), o_tok, s_new in zip(ids, o_toks, s_news):
        o_ref[bi, :, sl(gi)] = o_tok
        s_ref[bi * n_groups + gi] = s_new

    @pl.when(ci == n_chunks - 1)
    def _():
        for bi, gi in ids:
            s = s_ref[bi * n_groups + gi]
            for h in range(RW_GROUP):
                hs = slice(h * RW_HEAD, (h + 1) * RW_HEAD)
                sout_ref[bi, gi * RW_GROUP + h] = s[hs, hs]


def _rwkv_chunked(r, k2, v, na, b, logw, *, bb):
    bsz, s, rd = r.shape
    n_groups = rd // RW_NG
    n_chunks = s // RW_CHUNK
    n_heads = rd // RW_HEAD
    masks = jnp.asarray(_rwkv_masks())
    tri = jnp.asarray(np.tril(np.ones((RW_CHUNK, RW_CHUNK), np.float32))).astype(BF16)
    tok = pl.BlockSpec((bb, RW_CHUNK, rd), lambda bi, ci: (bi, ci, 0))
    return pl.pallas_call(
        functools.partial(_rwkv_chunk_kernel, bb=bb, n_groups=n_groups, n_chunks=n_chunks),
        grid=(bsz // bb, n_chunks),
        in_specs=[tok] * 6 + [pl.BlockSpec(masks.shape, lambda bi, ci: (0, 0, 0)),
                              pl.BlockSpec(tri.shape, lambda bi, ci: (0, 0))],
        out_specs=[tok, pl.BlockSpec((bb, n_heads, RW_HEAD, RW_HEAD), lambda bi, ci: (bi, 0, 0, 0))],
        out_shape=[jax.ShapeDtypeStruct((bsz, s, rd), F32),
                   jax.ShapeDtypeStruct((bsz, n_heads, RW_HEAD, RW_HEAD), F32)],
        scratch_shapes=[pltpu.VMEM((bb * n_groups, RW_NG, RW_NG), F32)],
        compiler_params=_cparams(("parallel", "arbitrary")),
        name="rwkv_chunked",
    )(r, k2, v, na, b, logw, masks, tri)


def _rwkv_step_kernel(s_ref, r_ref, k_ref, na_ref, b_ref, lw_ref, v_ref, sout_ref, o_ref):
    s = s_ref[0]
    sa = jnp.sum(s * na_ref[0], axis=-1, keepdims=True)
    s_new = s * jnp.exp(lw_ref[0]) + sa * b_ref[0] + v_ref[0] * k_ref[0]
    sout_ref[0] = s_new
    o_ref[0] = jnp.sum(s_new * r_ref[0], axis=-1, keepdims=True)


def _rwkv_step(state, r, k2, v, na, b, logw):
    db, nh, hd, _ = state.shape
    rowv = lambda x: x.reshape(db, nh, 1, hd)
    rspec = pl.BlockSpec((1, nh, 1, hd), lambda i: (i, 0, 0, 0))
    cspec = pl.BlockSpec((1, nh, hd, 1), lambda i: (i, 0, 0, 0))
    sspec = pl.BlockSpec((1, nh, hd, hd), lambda i: (i, 0, 0, 0))
    s_new, o = pl.pallas_call(
        _rwkv_step_kernel,
        grid=(db,),
        in_specs=[sspec, rspec, rspec, rspec, rspec, rspec, cspec],
        out_specs=[sspec, cspec],
        out_shape=[jax.ShapeDtypeStruct(state.shape, F32), jax.ShapeDtypeStruct((db, nh, hd, 1), F32)],
        compiler_params=_cparams(("parallel",)),
        name="rwkv_step",
    )(state, rowv(r), rowv(k2), rowv(na), rowv(b), rowv(logw), v.reshape(db, nh, hd, 1))
    return o.reshape(db, nh * hd), s_new


def _outproj_ln_kernel(oda_ref, orw_ref, bonus_ref, gate_ref, h_ref, w_ref, lg_ref, lb_ref,
                       ones_ref, g_ref, b_ref, o_ref, *, alpha):
    o = orw_ref[...]
    ones_bd = ones_ref[...]
    inv_n = 1.0 / RW_HEAD
    mu = _dot_exact_rhs(o, ones_bd) * inv_n
    d = o - mu
    var = _dot_exact_rhs(d * d, ones_bd) * inv_n
    o = d * lax.rsqrt(var + RW_GN_EPS) * lg_ref[...] + lb_ref[...]
    o_rw = (o + bonus_ref[...]) * gate_ref[...]
    da = oda_ref.shape[1]
    mix = (jnp.dot(oda_ref[...], w_ref[0:da, :], preferred_element_type=F32)
           + jnp.dot(o_rw.astype(BF16), w_ref[da:, :], preferred_element_type=F32))
    o_ref[...] = _layernorm(alpha * h_ref[...] + mix, g_ref[...], b_ref[...])


def _outproj_ln(oda, orw, bonus, gate, h, w_out, lnx_g, lnx_b, ones_bd, g, b, *, alpha, tm):
    m, d = h.shape
    da = oda.shape[1]
    rd = orw.shape[1]
    tok = lambda n: pl.BlockSpec((tm, n), lambda i: (i, 0))
    consts = [w_out, lnx_g, lnx_b, ones_bd, g, b]
    return pl.pallas_call(
        functools.partial(_outproj_ln_kernel, alpha=alpha),
        grid=(m // tm,),
        in_specs=[tok(da), tok(rd), tok(rd), tok(rd), tok(d)] + _weight_specs(consts, 1),
        out_specs=tok(d),
        out_shape=jax.ShapeDtypeStruct((m, d), F32),
        compiler_params=_cparams(("parallel",)),
        name="outproj_ln",
    )(oda, orw, bonus, gate, h, *consts)


def _tile(m, pref):
    return pref if m % pref == 0 else m


def kernel(x_prompt, x_sample, cache_k, cache_v, state_wkv, state_shift, page_table, ffa_w_gu, ffa_w_down, ffb_w_gu, ffb_w_down, ln1_g, ln1_b, ln2_g, ln2_b, ln3_g, ln3_b, w_in, w_out, da_lq1, da_lk1, da_lq2, da_lk2, da_subln_g, rw_mu, rw_w0, rw_w2, rw_a0, rw_a2, rw_g2, rw_k_k, rw_k_a, rw_r_k, rw_lnx_g, rw_lnx_b):
    bsz, seq, d_model = x_prompt.shape
    db, dec_seq, _ = x_sample.shape
    assert dec_seq == 1, "the sample group is written for one new token per sequence"
    depth = w_in.shape[0]
    n_pool, page, n_heads, _ = cache_k.shape[1:]
    n_pages = page_table.shape[1]
    rd = rw_w0.shape[1]
    rw_heads = rd // RW_HEAD
    d_ff = ffa_w_down.shape[1]
    alpha = (2 * depth) ** 0.25
    width = n_heads * V7X_LANES

    cos_p, sin_p = _rope_tables(jnp.arange(seq))
    cos_s, sin_s = _rope_tables(jnp.full((db,), n_pages * page, jnp.int32))
    ones_bd = jnp.asarray(np.kron(np.eye(rw_heads, dtype=np.float32),
                                  np.ones((RW_HEAD, RW_HEAD), np.float32))).astype(BF16)

    m_p = bsz * seq
    tm_p = _tile(m_p, 512)
    tf = _tile(d_ff, 256)
    y_p = x_prompt.reshape(m_p, d_model)
    y_s = x_sample.reshape(db, d_model)
    outs = [[] for _ in range(8)]
    row = lambda a: a[None, :]

    for l in range(depth):
        lam_init = 0.8 - 0.6 * math.exp(-0.3 * l)
        wa_gu, wa_dn = ffa_w_gu[l].astype(BF16), ffa_w_down[l].astype(BF16)
        wb_gu, wb_dn = ffb_w_gu[l].astype(BF16), ffb_w_down[l].astype(BF16)
        wi, wo = w_in[l].astype(BF16), w_out[l].astype(BF16)
        lam_vecs = [row(da_lq1[l]), row(da_lk1[l]), row(da_lq2[l]), row(da_lk2[l]), row(da_subln_g[l])]
        rw_weights = _rwkv_weights(dict(rw_mu=rw_mu[l], rw_w0=rw_w0[l], rw_w2=rw_w2[l], rw_a0=rw_a0[l],
                                        rw_a2=rw_a2[l], rw_g2=rw_g2[l], rw_k_k=rw_k_k[l],
                                        rw_k_a=rw_k_a[l], rw_r_k=rw_r_k[l]), rd)
        ln = [row(a[l]) for a in (ln1_g, ln1_b, ln2_g, ln2_b, ln3_g, ln3_b)]
        lnx = [row(rw_lnx_g[l]), row(rw_lnx_b[l])]

        h1 = _ffn_ln(y_p, wa_gu, wa_dn, ln[0], ln[1], alpha=alpha, tm=tm_p, tf=tf)
        q, k, kb, v, vb, prw = _inproj(h1, wi, cos_p, sin_p, n_heads=n_heads, tm=tm_p)
        oda = _attn_prompt(q.reshape(bsz, seq, width), kb.reshape(bsz, seq, width),
                           vb.reshape(bsz, seq, width), *lam_vecs,
                           n_heads=n_heads, tq=512, rs=128, lam_init=lam_init)
        prw3 = prw.reshape(bsz, seq, -1)
        shift0 = jnp.zeros((bsz, 1, prw3.shape[-1]), F32)
        r_, k2_, v_, na_, b_, lw_, gate, bonus = _rwkv_prep_seq(prw3, shift0, rw_weights, ones_bd, tt=512)
        orw, wkv_p = _rwkv_chunked(r_, k2_, v_, na_, b_, lw_, bb=2)
        h2 = _outproj_ln(oda.reshape(m_p, width), orw.reshape(m_p, rd), bonus.reshape(m_p, rd),
                         gate.reshape(m_p, rd), h1, wo, *lnx, ones_bd, ln[2], ln[3],
                         alpha=alpha, tm=tm_p)
        y_p = _ffn_ln(h2, wb_gu, wb_dn, ln[4], ln[5], alpha=alpha, tm=tm_p, tf=tf)
        outs[0].append(k.reshape(bsz, seq, n_heads, 2 * DA_QK))
        outs[1].append(v.reshape(bsz, seq, n_heads, DA_V))
        outs[2].append(wkv_p)
        outs[3].append(prw3[:, seq - 1:seq, :])

        h1 = _ffn_ln(y_s, wa_gu, wa_dn, ln[0], ln[1], alpha=alpha, tm=db, tf=tf)
        q, k, kb, v, vb, prw = _inproj(h1, wi, cos_s, sin_s, n_heads=n_heads, tm=db)
        oda = _attn_decode(q, kb, vb, cache_k.reshape(depth * n_pool, page * n_heads, V7X_LANES),
                           cache_v.reshape(depth * n_pool, page * n_heads, DA_V), page_table, l * n_pool,
                           *lam_vecs, n_heads=n_heads, gp=8, lam_init=lam_init)
        r_, k2_, v_, na_, b_, lw_, gate, bonus = _rwkv_prep_tok(
            prw, state_shift[l].reshape(db, -1), rw_weights, ones_bd)
        orw, wkv_s = _rwkv_step(state_wkv[l], r_, k2_, v_, na_, b_, lw_)
        h2 = _outproj_ln(oda, orw, bonus, gate, h1, wo, *lnx, ones_bd, ln[2], ln[3],
                         alpha=alpha, tm=db)
        y_s = _ffn_ln(h2, wb_gu, wb_dn, ln[4], ln[5], alpha=alpha, tm=db, tf=tf)
        outs[4].append(k.reshape(db, 1, n_heads, 2 * DA_QK))
        outs[5].append(v.reshape(db, 1, n_heads, DA_V))
        outs[6].append(wkv_s)
        outs[7].append(prw.reshape(db, 1, -1))

    stacked = [jnp.stack(o) for o in outs]
    return (y_p.reshape(bsz, seq, d_model), y_s.reshape(db, 1, d_model), *stacked)
```

```python
import functools
import math

import jax
import jax.numpy as jnp
import numpy as np
from jax import lax
from jax.experimental import pallas as pl
from jax.experimental.pallas import tpu as pltpu

F32 = jnp.float32
BF16 = jnp.bfloat16

DA_QK = 64
DA_V = 2 * DA_QK
RW_HEAD = 64
RW_W_LORA = 64
RW_A_LORA = 64
RW_G_LORA = 128
RW_GN_EPS = 64e-5
ROPE_THETA = 10000.0
LN_EPS = 1e-5
NEG = -1e30
LOG2E = math.log2(math.e)

V7X_LANES = 128
V7X_VMEM_LIMIT_BYTES = 56 * 1024 * 1024

RW_CHUNK = 64
RW_GROUP = 4
RW_NG = RW_CHUNK * RW_GROUP


def _cparams(semantics):
    return pltpu.CompilerParams(dimension_semantics=semantics,
                                vmem_limit_bytes=V7X_VMEM_LIMIT_BYTES)


def _dot(a, b):
    return jnp.dot(a.astype(BF16), b.astype(BF16), preferred_element_type=F32)


def _dot_nt(a, b):
    return lax.dot_general(a.astype(BF16), b.astype(BF16), (((1,), (1,)), ((), ())),
                           preferred_element_type=F32)


def _split(x):
    hi = x.astype(BF16)
    lo = (x - hi.astype(F32)).astype(BF16)
    return hi, lo


def _dot_exact_rhs(a, b):
    ah, al = _split(a)
    return (jnp.dot(ah, b, preferred_element_type=F32)
            + jnp.dot(al, b, preferred_element_type=F32))


def _layernorm(y, g, b):
    mu = jnp.mean(y, axis=-1, keepdims=True)
    d = y - mu
    var = jnp.mean(d * d, axis=-1, keepdims=True)
    return d * lax.rsqrt(var + LN_EPS) * g + b


def _ffn_ln_kernel(x_ref, wgu_ref, wd_ref, g_ref, b_ref, o_ref, *, d_ff, tf, alpha):
    x = x_ref[...]
    xb = x.astype(BF16)

    def activation(c):
        gate = jnp.dot(xb, wgu_ref[:, c * tf:(c + 1) * tf], preferred_element_type=F32)
        up = jnp.dot(xb, wgu_ref[:, d_ff + c * tf:d_ff + (c + 1) * tf], preferred_element_type=F32)
        return (gate * jax.nn.sigmoid(gate) * up).astype(BF16)

    nf = d_ff // tf
    act = activation(0)
    acc = None
    for c in range(nf):
        nxt = activation(c + 1) if c + 1 < nf else None
        part = jnp.dot(act, wd_ref[c * tf:(c + 1) * tf, :], preferred_element_type=F32)
        acc = part if acc is None else acc + part
        act = nxt
    o_ref[...] = _layernorm(alpha * x + 0.5 * acc, g_ref[...], b_ref[...])


def _ffn_ln(x, w_gu, w_down, g, b, *, alpha, tm, tf):
    m, d = x.shape
    f = w_down.shape[0]
    assert m % tm == 0 and f % tf == 0
    const = lambda shape: pl.BlockSpec(shape, lambda i: (0, 0))
    return pl.pallas_call(
        functools.partial(_ffn_ln_kernel, d_ff=f, tf=tf, alpha=alpha),
        grid=(m // tm,),
        in_specs=[pl.BlockSpec((tm, d), lambda i: (i, 0)), const(w_gu.shape), const(w_down.shape),
                  const((1, d)), const((1, d))],
        out_specs=pl.BlockSpec((tm, d), lambda i: (i, 0)),
        out_shape=jax.ShapeDtypeStruct((m, d), F32),
        compiler_params=_cparams(("parallel",)),
        name="ffn_ln",
    )(x, w_gu, w_down, g, b)


def _inproj_kernel(h_ref, w_ref, cos_ref, sin_ref,
                   q_ref, k_ref, kb_ref, v_ref, vb_ref, prw_ref, *, n_heads, scale):
    hb = h_ref[...].astype(BF16)
    tm = hb.shape[0]
    cos = cos_ref[...]
    sin = sin_ref[...]
    lane = lax.broadcasted_iota(jnp.int32, cos.shape, 1)
    first_half = (lane % DA_QK) < (DA_QK // 2)
    qc = n_heads * 2 * DA_QK

    def rope_head(x):
        rot = jnp.where(first_half, pltpu.roll(x, V7X_LANES - DA_QK // 2, 1),
                        pltpu.roll(x, DA_QK // 2, 1))
        return x * cos + rot * sin

    q = jnp.dot(hb, w_ref[:, 0:qc], preferred_element_type=F32)
    k = jnp.dot(hb, w_ref[:, qc:2 * qc], preferred_element_type=F32)
    for h in range(n_heads):
        sl = slice(h * V7X_LANES, (h + 1) * V7X_LANES)
        qh = rope_head(q[:, sl])
        kh = rope_head(k[:, sl])
        q_ref[:, sl] = (qh * scale).astype(BF16)
        kb_ref[:, sl] = kh.astype(BF16)
        k_ref[pl.ds(h, tm, stride=n_heads), :] = kh
    vc = n_heads * DA_V
    v = jnp.dot(hb, w_ref[:, 2 * qc:2 * qc + vc], preferred_element_type=F32)
    for h in range(n_heads):
        v_ref[pl.ds(h, tm, stride=n_heads), :] = v[:, h * DA_V:(h + 1) * DA_V]
    vb_ref[...] = v.astype(BF16)
    prw_ref[...] = jnp.dot(hb, w_ref[:, 2 * qc + vc:], preferred_element_type=F32)


def _inproj(h, w_in, cos_tab, sin_tab, *, n_heads, tm):
    m, d = h.shape
    cols = w_in.shape[1]
    qc = n_heads * 2 * DA_QK
    vc = n_heads * DA_V
    rw = cols - 2 * qc - vc
    n_pos_blocks = cos_tab.shape[0] // tm
    tok = lambda i: (i, 0)
    pos = lambda i: (i % n_pos_blocks, 0)
    return pl.pallas_call(
        functools.partial(_inproj_kernel, n_heads=n_heads, scale=DA_QK ** -0.5 * LOG2E),
        grid=(m // tm,),
        in_specs=[
            pl.BlockSpec((tm, d), tok),
            pl.BlockSpec((d, cols), lambda i: (0, 0)),
            pl.BlockSpec((tm, V7X_LANES), pos),
            pl.BlockSpec((tm, V7X_LANES), pos),
        ],
        out_specs=[
            pl.BlockSpec((tm, qc), tok), pl.BlockSpec((tm * n_heads, 2 * DA_QK), tok),
            pl.BlockSpec((tm, qc), tok), pl.BlockSpec((tm * n_heads, DA_V), tok),
            pl.BlockSpec((tm, vc), tok), pl.BlockSpec((tm, rw), tok),
        ],
        out_shape=[
            jax.ShapeDtypeStruct((m, qc), BF16), jax.ShapeDtypeStruct((m * n_heads, 2 * DA_QK), F32),
            jax.ShapeDtypeStruct((m, qc), BF16), jax.ShapeDtypeStruct((m * n_heads, DA_V), F32),
            jax.ShapeDtypeStruct((m, vc), BF16), jax.ShapeDtypeStruct((m, rw), F32),
        ],
        compiler_params=_cparams(("parallel",)),
        name="inproj_rope",
    )(h, w_in, cos_tab, sin_tab)


def _rope_tables(positions):
    half = DA_QK // 2
    inv = ROPE_THETA ** (-jnp.arange(half, dtype=F32) / half)
    ang = positions.astype(F32)[:, None] * inv[None, :]
    cos = jnp.cos(ang)
    sin = jnp.sin(ang)
    cos_tab = jnp.concatenate([cos, cos, cos, cos], axis=1)
    sin_tab = jnp.concatenate([-sin, sin, -sin, sin], axis=1)
    return cos_tab, sin_tab


def _lambda(lq1_ref, lk1_ref, lq2_ref, lk2_ref, lam_init):
    s1 = jnp.sum(lq1_ref[...] * lk1_ref[...], axis=-1, keepdims=True)
    s2 = jnp.sum(lq2_ref[...] * lk2_ref[...], axis=-1, keepdims=True)
    return jnp.exp(s1) - jnp.exp(s2) + lam_init


def _subln(o, g, lam_init):
    return o * lax.rsqrt(jnp.mean(o * o, axis=-1, keepdims=True) + LN_EPS) * g * (1.0 - lam_init)


def _attn_prompt_kernel(lq1_ref, lk1_ref, lq2_ref, lk2_ref, g_ref, q_ref, k_ref, v_ref, o_ref,
                        qm_ref, vx_ref, s_ref, p_ref, m_ref, acc_ref, *, tq, rs, lam_init):
    i = pl.program_id(2)

    @pl.when(i == 0)
    def _():
        vx_ref[:, 0:DA_V] = v_ref[0]
        vx_ref[:, DA_V:] = jnp.ones((vx_ref.shape[0], V7X_LANES), BF16)

    q = q_ref[0].astype(F32)
    lane = lax.broadcasted_iota(jnp.int32, q.shape, 1)
    qm_ref[0] = jnp.where(lane < DA_QK, q, 0.0).astype(BF16)
    qm_ref[1] = jnp.where(lane >= DA_QK, q, 0.0).astype(BF16)
    m_ref[...] = jnp.full(m_ref.shape, NEG, F32)
    acc_ref[...] = jnp.zeros(acc_ref.shape, F32)
    nt = tq // V7X_LANES

    kw = 2 * V7X_LANES

    def score_parts(j, buf):
        def part(mp, t):
            kblk = k_ref[0, pl.ds(pl.multiple_of(j * tq + t * kw, kw), kw), :]
            s_ref[buf, mp, :, t * kw:(t + 1) * kw] = lax.dot_general(
                qm_ref[mp], kblk, (((1,), (1,)), ((), ())), preferred_element_type=F32)
        return [functools.partial(part, mp, t) for t in range(tq // kw) for mp in range(2)]

    def scores(j, buf):
        for part in score_parts(j, buf):
            part()

    def softmax_pv(j, buf, diagonal, fillers=()):
        off = pl.multiple_of(j * tq, tq)
        fillers = list(fillers)
        for r in range(tq // rs):
            if fillers:
                fillers.pop(0)()
            rows = slice(r * rs, (r + 1) * rs)
            n_vis = min(nt, pl.cdiv((r + 1) * rs, V7X_LANES)) if diagonal else nt
            vblk = vx_ref[pl.ds(off, n_vis * V7X_LANES), :]
            def tile(mp, t):
                s = s_ref[buf, mp, rows, t * V7X_LANES:(t + 1) * V7X_LANES]
                if diagonal and (t + 1) * V7X_LANES > r * rs + 1:
                    row = lax.broadcasted_iota(jnp.int32, s.shape, 0) + r * rs
                    col = lax.broadcasted_iota(jnp.int32, s.shape, 1) + t * V7X_LANES
                    s = jnp.where(col <= row, s, NEG)
                return s

            for mp in range(2):
                mx = functools.reduce(jnp.maximum, [tile(mp, t) for t in range(n_vis)])
                m_old = m_ref[mp, rows, :]
                m_new = jnp.maximum(m_old, jnp.max(mx, axis=1, keepdims=True))
                for t in range(n_vis):
                    p_ref[mp, rows, t * V7X_LANES:(t + 1) * V7X_LANES] = jnp.exp2(
                        tile(mp, t) - m_new).astype(BF16)
                c = jnp.exp2(m_old - m_new)
                m_ref[mp, rows, :] = m_new
                pv = jnp.dot(p_ref[mp, rows, 0:n_vis * V7X_LANES], vblk, preferred_element_type=F32)
                acc_ref[mp, rows, :] = jnp.concatenate([c, c], axis=1) * acc_ref[mp, rows, :] + pv

    scores(0, 0)

    def body(jj, carry):
        j = 2 * jj
        softmax_pv(j, 0, False, score_parts(j + 1, 1))
        softmax_pv(j + 1, 1, False, score_parts(j + 2, 0))
        return carry

    lax.fori_loop(0, i // 2, body, 0)

    @pl.when(i % 2 == 0)
    def _():
        softmax_pv(i, 0, True)

    @pl.when(i % 2 == 1)
    def _():
        softmax_pv(i - 1, 0, False, score_parts(i, 1))
        softmax_pv(i, 1, True)

    lam = _lambda(lq1_ref, lk1_ref, lq2_ref, lk2_ref, lam_init)
    o = (acc_ref[0, :, 0:DA_V] / acc_ref[0, :, DA_V:]
         - lam * (acc_ref[1, :, 0:DA_V] / acc_ref[1, :, DA_V:]))
    o_ref[0] = _subln(o, g_ref[...], lam_init).astype(o_ref.dtype)


def _attn_prompt(q, kb, vb, lq1, lk1, lq2, lk2, subln_g, *, n_heads, tq, rs, lam_init):
    b, s, _ = q.shape
    vec = lambda n: pl.BlockSpec((1, n), lambda bi, h, i: (0, 0))
    return pl.pallas_call(
        functools.partial(_attn_prompt_kernel, tq=tq, rs=rs, lam_init=lam_init),
        grid=(b, n_heads, s // tq),
        in_specs=[
            vec(DA_QK), vec(DA_QK), vec(DA_QK), vec(DA_QK), vec(DA_V),
            pl.BlockSpec((1, tq, V7X_LANES), lambda bi, h, i: (bi, i, h)),
            pl.BlockSpec((1, s, V7X_LANES), lambda bi, h, i: (bi, 0, h)),
            pl.BlockSpec((1, s, DA_V), lambda bi, h, i: (bi, 0, h)),
        ],
        out_specs=pl.BlockSpec((1, tq, DA_V), lambda bi, h, i: (bi, i, h)),
        out_shape=jax.ShapeDtypeStruct((b, s, n_heads * DA_V), BF16),
        scratch_shapes=[pltpu.VMEM((2, tq, V7X_LANES), BF16), pltpu.VMEM((s, DA_V + V7X_LANES), BF16),
                        pltpu.VMEM((2, 2, tq, tq), F32), pltpu.VMEM((2, tq, tq), BF16),
                        pltpu.VMEM((2, tq, V7X_LANES), F32),
                        pltpu.VMEM((2, tq, DA_V + V7X_LANES), F32)],
        compiler_params=_cparams(("parallel", "parallel", "arbitrary")),
        name="attn_prompt",
    )(lq1, lk1, lq2, lk2, subln_g, q, kb, vb)


def _attn_decode_kernel(pt_ref, lq1_ref, lk1_ref, lq2_ref, lk2_ref, g_ref, q_ref, kn_ref, vn_ref,
                        *rest, n_heads, n_steps, gp, lam_init):
    del pt_ref
    kp_refs, vp_refs = rest[:gp], rest[gp:2 * gp]
    o_ref, qm_ref, ex_ref, m_ref, l_ref, acc_ref = rest[2 * gp:]
    p = pl.program_id(1)
    nj = 2 * n_heads
    dn = (((1,), (1,)), ((), ()))
    sub = lax.broadcasted_iota(jnp.int32, (8, V7X_LANES), 0)
    lane = lax.broadcasted_iota(jnp.int32, (8, V7X_LANES), 1)
    live = (sub % n_heads) == (lane // 2)

    def head_rows(x):
        out = jnp.zeros((8, V7X_LANES), F32)
        for h in range(n_heads):
            xh = jnp.broadcast_to(x[:, h * V7X_LANES:(h + 1) * V7X_LANES], (8, V7X_LANES))
            out = jnp.where(sub == h, xh, out)
        return out

    def per_row(x, mp):
        pick = lane == 2 * (sub % n_heads) + mp
        return jnp.broadcast_to(jnp.sum(jnp.where(pick, x, 0.0), axis=1, keepdims=True),
                                (8, V7X_LANES))

    def softmax_rows(s, rows_live):
        r = s.shape[0]
        s3 = jnp.where(rows_live, s.reshape(r // 8, 8, V7X_LANES), NEG)
        m_g = jnp.max(jnp.max(s3, axis=0), axis=0, keepdims=True)
        pr = jnp.exp2(s3 - m_g).reshape(r, V7X_LANES)
        return m_g, pr, jnp.sum(pr, axis=0, keepdims=True)

    def weighted_values(pb, v):
        r = v.shape[0]
        return [jnp.sum((pb[:, mp * V7X_LANES:(mp + 1) * V7X_LANES] * v)
                        .reshape(r // 8, 8, V7X_LANES), axis=0) for mp in range(2)]

    @pl.when(p == 0)
    def _():
        e_r = lax.broadcasted_iota(jnp.int32, ex_ref.shape, 0)
        e_c = lax.broadcasted_iota(jnp.int32, ex_ref.shape, 1)
        ex_ref[...] = jnp.where((e_r < nj) & (e_r % 2 == e_c // V7X_LANES), 1.0, 0.0).astype(BF16)
        qrows = jnp.zeros((8, V7X_LANES), F32)
        q = q_ref[0].astype(F32)
        for h in range(n_heads):
            qh = jnp.broadcast_to(q[:, h * V7X_LANES:(h + 1) * V7X_LANES], (8, V7X_LANES))
            qrows = jnp.where(sub // 2 == h, qh, qrows)
        qrows = jnp.where(lane // DA_QK == sub % 2, qrows, 0.0)
        qm_ref[...] = jnp.zeros(qm_ref.shape, BF16)
        qm_ref[0:8, :] = qrows.astype(BF16)
        kn = head_rows(kn_ref[0].astype(F32))
        s_new = lax.dot_general(kn.astype(BF16), qm_ref[...], dn, preferred_element_type=F32)
        m_g, pr, l_g = softmax_rows(s_new, live & (sub < n_heads))
        m_ref[...] = m_g
        l_ref[...] = l_g
        pb = jnp.dot(pr.astype(BF16), ex_ref[...], preferred_element_type=F32)
        parts = weighted_values(pb, head_rows(vn_ref[0].astype(F32)))
        for mp in range(2):
            acc_ref[mp] = parts[mp]

    qm = qm_ref[...]
    s_pg = [lax.dot_general(kp_refs[g][0].astype(BF16), qm, dn, preferred_element_type=F32)
            for g in range(gp)]
    stats = [softmax_rows(s, live) for s in s_pg]
    ex = ex_ref[...]
    pb_pg = [jnp.dot(st[1].astype(BF16), ex, preferred_element_type=F32) for st in stats]
    part_pg = [weighted_values(pb, vp_refs[g][0]) for g, pb in enumerate(pb_pg)]
    m_old = m_ref[...]
    m_new = functools.reduce(jnp.maximum, [st[0] for st in stats], m_old)
    c = jnp.exp2(m_old - m_new)
    w_pg = [jnp.exp2(st[0] - m_new) for st in stats]
    l_ref[...] = c * l_ref[...] + functools.reduce(
        jnp.add, [w * st[2] for w, st in zip(w_pg, stats)])
    m_ref[...] = m_new
    for mp in range(2):
        acc = per_row(c, mp) * acc_ref[mp]
        for g in range(gp):
            acc = acc + per_row(w_pg[g], mp) * part_pg[g][mp]
        acc_ref[mp] = acc

    @pl.when(p == n_steps - 1)
    def _():
        lam = _lambda(lq1_ref, lk1_ref, lq2_ref, lk2_ref, lam_init)
        on = [acc_ref[mp] / per_row(l_ref[...], mp) for mp in range(2)]
        for h in range(n_heads):
            o1, o2 = [functools.reduce(jnp.add, [x[i:i + 1, :] for i in range(h, 8, n_heads)])
                      for x in on]
            o_ref[0, :, h * V7X_LANES:(h + 1) * V7X_LANES] = _subln(
                o1 - lam * o2, g_ref[...], lam_init).astype(o_ref.dtype)


def _attn_decode(q, kb, vb, cache_k, cache_v, page_table, page_base, lq1, lk1, lq2, lk2, subln_g,
                 *, n_heads, gp, lam_init):
    db, width = q.shape
    n_pages = page_table.shape[1]
    rows = cache_k.shape[1]
    assert 8 % n_heads == 0 and rows % 8 == 0
    n_steps = n_pages // gp
    vec = lambda n: pl.BlockSpec((1, n), lambda bi, p, pt: (0, 0))
    tok = pl.BlockSpec((1, 1, width), lambda bi, p, pt: (bi, 0, 0))
    pages = [pl.BlockSpec((1, rows, V7X_LANES),
                          lambda bi, p, pt, g=g: (page_base + pt[bi, p * gp + g], 0, 0))
             for g in range(gp)]
    grid_spec = pltpu.PrefetchScalarGridSpec(
        num_scalar_prefetch=1,
        grid=(db, n_steps),
        in_specs=[vec(DA_QK), vec(DA_QK), vec(DA_QK), vec(DA_QK), vec(DA_V), tok, tok, tok]
                 + pages + pages,
        out_specs=tok,
        scratch_shapes=[pltpu.VMEM((V7X_LANES, V7X_LANES), BF16),
                        pltpu.VMEM((V7X_LANES, 2 * V7X_LANES), BF16),
                        pltpu.VMEM((1, V7X_LANES), F32), pltpu.VMEM((1, V7X_LANES), F32),
                        pltpu.VMEM((2, 8, V7X_LANES), F32)],
    )
    out = pl.pallas_call(
        functools.partial(_attn_decode_kernel, n_heads=n_heads, n_steps=n_steps, gp=gp,
                          lam_init=lam_init),
        grid_spec=grid_spec,
        out_shape=jax.ShapeDtypeStruct((db, 1, width), BF16),
        compiler_params=_cparams(("parallel", "arbitrary")),
        name="attn_decode",
    )(page_table, lq1, lk1, lq2, lk2, subln_g,
      q.reshape(db, 1, width), kb.reshape(db, 1, width), vb.reshape(db, 1, width),
      *([cache_k] * gp), *([cache_v] * gp))
    return out.reshape(db, width)


def _rwkv_prep_math(x, prev, mu, w0, w2p, a0, a2p, g2, k_k, k_a, r_k, ones_bd):
    rd = ones_bd.shape[0]
    xm = x + (prev - x) * mu
    r = xm[:, 0:rd]
    k = xm[:, rd:2 * rd]
    v = xm[:, 2 * rd:3 * rd]
    xwa = xm[:, 3 * rd:3 * rd + RW_W_LORA + RW_A_LORA]
    xg = xm[:, 3 * rd + RW_W_LORA + RW_A_LORA:]
    zw = -(w0 + _dot(jnp.tanh(xwa), w2p))
    softplus = jnp.maximum(zw, 0.0) + jnp.log(1.0 + jnp.exp(-jnp.abs(zw)))
    logw = -jnp.exp(-softplus - 0.5)
    a = jax.nn.sigmoid(a0 + _dot(xwa, a2p))
    g = _dot(jax.nn.sigmoid(xg), g2)
    kk = k * k_k
    n2 = _dot_exact_rhs(kk * kk, ones_bd)
    kk = kk / jnp.maximum(jnp.sqrt(n2), 1e-12)
    k2 = k * (1.0 + (a - 1.0) * k_a)
    bonus = _dot_exact_rhs(r * k2 * r_k, ones_bd) * v
    return r, k2, v, -kk, kk * a, logw, g, bonus


def _rwkv_prep_seq_kernel(x_ref, shift_ref, mu_ref, w0_ref, w2p_ref, a0_ref, a2p_ref, g2_ref,
                          kk_ref, ka_ref, rk_ref, ones_ref, *rest):
    outs, carry_ref = rest[:-1], rest[-1]
    t = pl.program_id(1)

    @pl.when(t == 0)
    def _():
        carry_ref[...] = shift_ref[0]

    x = x_ref[0]
    tt = x.shape[0]
    row = lax.broadcasted_iota(jnp.int32, x.shape, 0)
    prev = jnp.where(row == 0, carry_ref[...], pltpu.roll(x, 1, 0))
    carry_ref[...] = x[tt - 1:tt, :]
    vals = _rwkv_prep_math(x, prev, mu_ref[...], w0_ref[...], w2p_ref[...], a0_ref[...],
                           a2p_ref[...], g2_ref[...], kk_ref[...], ka_ref[...], rk_ref[...],
                           ones_ref[...])
    for o_ref, val in zip(outs, vals):
        o_ref[0] = val.astype(o_ref.dtype)


def _rwkv_prep_tok_kernel(x_ref, prev_ref, mu_ref, w0_ref, w2p_ref, a0_ref, a2p_ref, g2_ref,
                          kk_ref, ka_ref, rk_ref, ones_ref, *outs):
    vals = _rwkv_prep_math(x_ref[...], prev_ref[...], mu_ref[...], w0_ref[...], w2p_ref[...],
                           a0_ref[...], a2p_ref[...], g2_ref[...], kk_ref[...], ka_ref[...],
                           rk_ref[...], ones_ref[...])
    for o_ref, val in zip(outs, vals):
        o_ref[...] = val


def _rwkv_weights(p, rd):
    zeros_w = jnp.zeros((RW_A_LORA, rd), F32)
    zeros_a = jnp.zeros((RW_W_LORA, rd), F32)
    return [p['rw_mu'][None, :], p['rw_w0'][None, :],
            jnp.concatenate([p['rw_w2'], zeros_w], axis=0).astype(BF16), p['rw_a0'][None, :],
            jnp.concatenate([zeros_a, p['rw_a2']], axis=0).astype(BF16), p['rw_g2'].astype(BF16),
            p['rw_k_k'][None, :], p['rw_k_a'][None, :], p['rw_r_k'].reshape(1, rd)]


def _weight_specs(weights, nargs):
    zero = {1: lambda i: (0, 0), 2: lambda i, j: (0, 0)}[nargs]
    return [pl.BlockSpec(w.shape, zero) for w in weights]


def _rwkv_prep_seq(prw, shift, weights, ones_bd, *, tt):
    b, s, c = prw.shape
    rd = ones_bd.shape[0]
    consts = weights + [ones_bd]
    out_spec = pl.BlockSpec((1, tt, rd), lambda bi, t: (bi, t, 0))
    return pl.pallas_call(
        _rwkv_prep_seq_kernel,
        grid=(b, s // tt),
        in_specs=[pl.BlockSpec((1, tt, c), lambda bi, t: (bi, t, 0)),
                  pl.BlockSpec((1, 1, c), lambda bi, t: (bi, 0, 0))] + _weight_specs(consts, 2),
        out_specs=[out_spec] * 8,
        out_shape=[jax.ShapeDtypeStruct((b, s, rd), F32 if i == 5 else BF16) for i in range(8)],
        scratch_shapes=[pltpu.VMEM((1, c), F32)],
        compiler_params=_cparams(("parallel", "arbitrary")),
        name="rwkv_prep_seq",
    )(prw, shift, *consts)


def _rwkv_prep_tok(prw, prev, weights, ones_bd):
    m, c = prw.shape
    rd = ones_bd.shape[0]
    consts = weights + [ones_bd]
    return pl.pallas_call(
        _rwkv_prep_tok_kernel,
        grid=(1,),
        in_specs=[pl.BlockSpec((m, c), lambda i: (0, 0))] * 2 + _weight_specs(consts, 1),
        out_specs=[pl.BlockSpec((m, rd), lambda i: (0, 0))] * 8,
        out_shape=[jax.ShapeDtypeStruct((m, rd), F32)] * 8,
        compiler_params=_cparams(("arbitrary",)),
        name="rwkv_prep_tok",
    )(prw, prev, *consts)


def _rwkv_masks():
    n = RW_NG
    row = np.arange(n)[:, None]
    col = np.arange(n)[None, :]
    same = (row // RW_CHUNK) == (col // RW_CHUNK)
    masks = [same & (col < row), same & (col <= row), (row // 2 == col // 2) & (col < row)]
    s = 2
    while s < RW_CHUNK:
        masks.append(((row // s) % 2 == 1) & ((col // s) == (row // s) - 1))
        s *= 2
    return np.stack(masks).astype(np.float32)


def _rwkv_chunk_chains(chains, masks_ref, tri, s_olds):
    c, ng = chains[0][0].shape
    lane = lax.broadcasted_iota(jnp.int32, (c, ng), 1)
    row = lax.broadcasted_iota(jnp.int32, (ng, ng), 0)
    col = lax.broadcasted_iota(jnp.int32, (ng, ng), 1)
    d = functools.partial(jnp.dot, preferred_element_type=F32)
    cat0 = lambda *xs: jnp.concatenate(xs, axis=0)
    cat1 = lambda *xs: jnp.concatenate(xs, axis=1)

    def stack(x):
        return cat0(*[jnp.where(lane // RW_HEAD == h, x, 0.0) for h in range(ng // RW_HEAD)])

    def split3(logw):
        hi = logw.astype(BF16)
        rem = logw - hi.astype(F32)
        mid = rem.astype(BF16)
        return hi, mid, (rem - mid.astype(F32)).astype(BF16)

    parts = [split3(ch[5]) for ch in chains]
    cums = [d(tri, hi) + d(tri, mid) + d(tri, lo) for hi, mid, lo in parts]

    def decayed(ch, cum):
        r, k2, v, na, b, logw = ch
        cum_c = cum[c - 1:c, :]
        g_inv = jnp.exp(-cum)
        g_tail = jnp.exp(cum_c - cum)
        return dict(at=stack(na * jnp.exp(cum - logw)), rt=stack(r * jnp.exp(cum)),
                    btkt=cat0(stack(b * g_inv), stack(k2 * g_inv)),
                    bhkh=cat0(stack(b * g_tail), stack(k2 * g_tail)), vm=stack(v),
                    gc=jnp.exp(cum_c))

    q = [decayed(ch, cum) for ch, cum in zip(chains, cums)]
    grams = [_dot_nt(cat0(x['at'], x['rt']), x['btkt']) for x in q]
    strict = masks_ref[0]
    incl = masks_ref[1]
    ns = [g[:ng, :ng] * strict for g in grams]
    gaks = [g[:ng, ng:] * strict for g in grams]
    grbs = [g[ng:, :ng] * incl for g in grams]
    grks = [g[ng:, ng:] * incl for g in grams]

    eye = jnp.where(row == col, 1.0, 0.0)
    xs = [eye + n * masks_ref[2] for n in ns]
    for lvl in range(3, masks_ref.shape[0]):
        lvl_mask = masks_ref[lvl]
        ys = [_dot(x, n * lvl_mask) for x, n in zip(xs, ns)]
        xs = [x + _dot(y, x) for x, y in zip(xs, ys)]

    gvs = [_dot(gak, x['vm']) for gak, x in zip(gaks, q)]
    aus = [_dot(x, cat1(y['at'], gv)) for x, y, gv in zip(xs, q, gvs)]
    ts = [_dot(grb, au) for grb, au in zip(grbs, aus)]
    gkvs = [_dot(grk, x['vm']) for grk, x in zip(grks, q)]
    rbs = [x['rt'] + t[:, :ng] for x, t in zip(q, ts)]
    obs = [t[:, ng:] + gkv for t, gkv in zip(ts, gkvs)]
    ms = [jnp.where(row == col, x['gc'], 0.0) + _dot(au[:, :ng].T, x['bhkh'][:ng])
          for x, au in zip(q, aus)]
    zs = [_dot(cat0(au[:, ng:], x['vm']).T, x['bhkh']) for x, au in zip(q, aus)]

    os_ = [_dot_nt(rb, s_old) + ob for rb, s_old, ob in zip(rbs, s_olds, obs)]
    s_news = [_dot(s_old, m) + z for s_old, m, z in zip(s_olds, ms, zs)]
    o_toks = [functools.reduce(jnp.add, [o[h * c:(h + 1) * c] for h in range(ng // RW_HEAD)])
              for o in os_]
    return o_toks, s_news


def _rwkv_chunk_kernel(r_ref, k_ref, v_ref, na_ref, b_ref, lw_ref, masks_ref, tri_ref,
                       o_ref, sout_ref, s_ref, *, bb, n_groups, n_chunks):
    ci = pl.program_id(1)

    @pl.when(ci == 0)
    def _():
        s_ref[...] = jnp.zeros_like(s_ref)

    ids = [(bi, gi) for bi in range(bb) for gi in range(n_groups)]
    sl = lambda gi: slice(gi * RW_NG, (gi + 1) * RW_NG)
    chains = [tuple(ref[bi, :, sl(gi)].astype(F32)
                    for ref in (r_ref, k_ref, v_ref, na_ref, b_ref, lw_ref))
              for bi, gi in ids]
    o_toks, s_news = _rwkv_chunk_chains(chains, masks_ref, tri_ref[...],
                                        [s_ref[bi * n_groups + gi] for bi, gi in ids])
    for (bi, gi), o_tok, s_new in zip(ids, o_toks, s_news):
        o_ref[bi, :, sl(gi)] = o_tok
        s_ref[bi * n_groups + gi] = s_new

    @pl.when(ci == n_chunks - 1)
    def _():
        for bi, gi in ids:
            s = s_ref[bi * n_groups + gi]
            for h in range(RW_GROUP):
                hs = slice(h * RW_HEAD, (h + 1) * RW_HEAD)
                sout_ref[bi, gi * RW_GROUP + h] = s[hs, hs]


def _rwkv_chunked(r, k2, v, na, b, logw, *, bb):
    bsz, s, rd = r.shape
    n_groups = rd // RW_NG
    n_chunks = s // RW_CHUNK
    n_heads = rd // RW_HEAD
    masks = jnp.asarray(_rwkv_masks())
    tri = jnp.asarray(np.tril(np.ones((RW_CHUNK, RW_CHUNK), np.float32))).astype(BF16)
    tok = pl.BlockSpec((bb, RW_CHUNK, rd), lambda bi, ci: (bi, ci, 0))
    return pl.pallas_call(
        functools.partial(_rwkv_chunk_kernel, bb=bb, n_groups=n_groups, n_chunks=n_chunks),
        grid=(bsz // bb, n_chunks),
        in_specs=[tok] * 6 + [pl.BlockSpec(masks.shape, lambda bi, ci: (0, 0, 0)),
                              pl.BlockSpec(tri.shape, lambda bi, ci: (0, 0))],
        out_specs=[tok, pl.BlockSpec((bb, n_heads, RW_HEAD, RW_HEAD), lambda bi, ci: (bi, 0, 0, 0))],
        out_shape=[jax.ShapeDtypeStruct((bsz, s, rd), F32),
                   jax.ShapeDtypeStruct((bsz, n_heads, RW_HEAD, RW_HEAD), F32)],
        scratch_shapes=[pltpu.VMEM((bb * n_groups, RW_NG, RW_NG), F32)],
        compiler_params=_cparams(("parallel", "arbitrary")),
        name="rwkv_chunked",
    )(r, k2, v, na, b, logw, masks, tri)


def _rwkv_step_kernel(s_ref, r_ref, k_ref, na_ref, b_ref, lw_ref, v_ref, sout_ref, o_ref):
    s = s_ref[0]
    sa = jnp.sum(s * na_ref[0], axis=-1, keepdims=True)
    s_new = s * jnp.exp(lw_ref[0]) + sa * b_ref[0] + v_ref[0] * k_ref[0]
    sout_ref[0] = s_new
    o_ref[0] = jnp.sum(s_new * r_ref[0], axis=-1, keepdims=True)


def _rwkv_step(state, r, k2, v, na, b, logw):
    db, nh, hd, _ = state.shape
    rowv = lambda x: x.reshape(db, nh, 1, hd)
    rspec = pl.BlockSpec((1, nh, 1, hd), lambda i: (i, 0, 0, 0))
    cspec = pl.BlockSpec((1, nh, hd, 1), lambda i: (i, 0, 0, 0))
    sspec = pl.BlockSpec((1, nh, hd, hd), lambda i: (i, 0, 0, 0))
    s_new, o = pl.pallas_call(
        _rwkv_step_kernel,
        grid=(db,),
        in_specs=[sspec, rspec, rspec, rspec, rspec, rspec, cspec],
        out_specs=[sspec, cspec],
        out_shape=[jax.ShapeDtypeStruct(state.shape, F32), jax.ShapeDtypeStruct((db, nh, hd, 1), F32)],
        compiler_params=_cparams(("parallel",)),
        name="rwkv_step",
    )(state, rowv(r), rowv(k2), rowv(na), rowv(b), rowv(logw), v.reshape(db, nh, hd, 1))
    return o.reshape(db, nh * hd), s_new


def _outproj_ln_kernel(oda_ref, orw_ref, bonus_ref, gate_ref, h_ref, w_ref, lg_ref, lb_ref,
                       ones_ref, g_ref, b_ref, o_ref, *, alpha):
    o = orw_ref[...]
    ones_bd = ones_ref[...]
    inv_n = 1.0 / RW_HEAD
    mu = _dot_exact_rhs(o, ones_bd) * inv_n
    d = o - mu
    var = _dot_exact_rhs(d * d, ones_bd) * inv_n
    o = d * lax.rsqrt(var + RW_GN_EPS) * lg_ref[...] + lb_ref[...]
    o_rw = (o + bonus_ref[...]) * gate_ref[...]
    da = oda_ref.shape[1]
    mix = (jnp.dot(oda_ref[...], w_ref[0:da, :], preferred_element_type=F32)
           + jnp.dot(o_rw.astype(BF16), w_ref[da:, :], preferred_element_type=F32))
    o_ref[...] = _layernorm(alpha * h_ref[...] + mix, g_ref[...], b_ref[...])


def _outproj_ln(oda, orw, bonus, gate, h, w_out, lnx_g, lnx_b, ones_bd, g, b, *, alpha, tm):
    m, d = h.shape
    da = oda.shape[1]
    rd = orw.shape[1]
    tok = lambda n: pl.BlockSpec((tm, n), lambda i: (i, 0))
    consts = [w_out, lnx_g, lnx_b, ones_bd, g, b]
    return pl.pallas_call(
        functools.partial(_outproj_ln_kernel, alpha=alpha),
        grid=(m // tm,),
        in_specs=[tok(da), tok(rd), tok(rd), tok(rd), tok(d)] + _weight_specs(consts, 1),
        out_specs=tok(d),
        out_shape=jax.ShapeDtypeStruct((m, d), F32),
        compiler_params=_cparams(("parallel",)),
        name="outproj_ln",
    )(oda, orw, bonus, gate, h, *consts)


def _tile(m, pref):
    return pref if m % pref == 0 else m


def kernel(x_prompt, x_sample, cache_k, cache_v, state_wkv, state_shift, page_table, ffa_w_gu, ffa_w_down, ffb_w_gu, ffb_w_down, ln1_g, ln1_b, ln2_g, ln2_b, ln3_g, ln3_b, w_in, w_out, da_lq1, da_lk1, da_lq2, da_lk2, da_subln_g, rw_mu, rw_w0, rw_w2, rw_a0, rw_a2, rw_g2, rw_k_k, rw_k_a, rw_r_k, rw_lnx_g, rw_lnx_b):
    bsz, seq, d_model = x_prompt.shape
    db, dec_seq, _ = x_sample.shape
    assert dec_seq == 1, "the sample group is written for one new token per sequence"
    depth = w_in.shape[0]
    n_pool, page, n_heads, _ = cache_k.shape[1:]
    n_pages = page_table.shape[1]
    rd = rw_w0.shape[1]
    rw_heads = rd // RW_HEAD
    d_ff = ffa_w_down.shape[1]
    alpha = (2 * depth) ** 0.25
    width = n_heads * V7X_LANES

    cos_p, sin_p = _rope_tables(jnp.arange(seq))
    cos_s, sin_s = _rope_tables(jnp.full((db,), n_pages * page, jnp.int32))
    ones_bd = jnp.asarray(np.kron(np.eye(rw_heads, dtype=np.float32),
                                  np.ones((RW_HEAD, RW_HEAD), np.float32))).astype(BF16)

    m_p = bsz * seq
    tm_p = _tile(m_p, 512)
    tf = _tile(d_ff, 256)
    y_p = x_prompt.reshape(m_p, d_model)
    y_s = x_sample.reshape(db, d_model)
    outs = [[] for _ in range(8)]
    row = lambda a: a[None, :]

    for l in range(depth):
        lam_init = 0.8 - 0.6 * math.exp(-0.3 * l)
        wa_gu, wa_dn = ffa_w_gu[l].astype(BF16), ffa_w_down[l].astype(BF16)
        wb_gu, wb_dn = ffb_w_gu[l].astype(BF16), ffb_w_down[l].astype(BF16)
        wi, wo = w_in[l].astype(BF16), w_out[l].astype(BF16)
        lam_vecs = [row(da_lq1[l]), row(da_lk1[l]), row(da_lq2[l]), row(da_lk2[l]), row(da_subln_g[l])]
        rw_weights = _rwkv_weights(dict(rw_mu=rw_mu[l], rw_w0=rw_w0[l], rw_w2=rw_w2[l], rw_a0=rw_a0[l],
                                        rw_a2=rw_a2[l], rw_g2=rw_g2[l], rw_k_k=rw_k_k[l],
                                        rw_k_a=rw_k_a[l], rw_r_k=rw_r_k[l]), rd)
        ln = [row(a[l]) for a in (ln1_g, ln1_b, ln2_g, ln2_b, ln3_g, ln3_b)]
        lnx = [row(rw_lnx_g[l]), row(rw_lnx_b[l])]

        h1 = _ffn_ln(y_p, wa_gu, wa_dn, ln[0], ln[1], alpha=alpha, tm=tm_p, tf=tf)
        q, k, kb, v, vb, prw = _inproj(h1, wi, cos_p, sin_p, n_heads=n_heads, tm=tm_p)
        oda = _attn_prompt(q.reshape(bsz, seq, width), kb.reshape(bsz, seq, width),
                           vb.reshape(bsz, seq, width), *lam_vecs,
                           n_heads=n_heads, tq=512, rs=128, lam_init=lam_init)
        prw3 = prw.reshape(bsz, seq, -1)
        shift0 = jnp.zeros((bsz, 1, prw3.shape[-1]), F32)
        r_, k2_, v_, na_, b_, lw_, gate, bonus = _rwkv_prep_seq(prw3, shift0, rw_weights, ones_bd, tt=512)
        orw, wkv_p = _rwkv_chunked(r_, k2_, v_, na_, b_, lw_, bb=4)
        h2 = _outproj_ln(oda.reshape(m_p, width), orw.reshape(m_p, rd), bonus.reshape(m_p, rd),
                         gate.reshape(m_p, rd), h1, wo, *lnx, ones_bd, ln[2], ln[3],
                         alpha=alpha, tm=tm_p)
        y_p = _ffn_ln(h2, wb_gu, wb_dn, ln[4], ln[5], alpha=alpha, tm=tm_p, tf=tf)
        outs[0].append(k.reshape(bsz, seq, n_heads, 2 * DA_QK))
        outs[1].append(v.reshape(bsz, seq, n_heads, DA_V))
        outs[2].append(wkv_p)
        outs[3].append(prw3[:, seq - 1:seq, :])

        h1 = _ffn_ln(y_s, wa_gu, wa_dn, ln[0], ln[1], alpha=alpha, tm=db, tf=tf)
        q, k, kb, v, vb, prw = _inproj(h1, wi, cos_s, sin_s, n_heads=n_heads, tm=db)
        oda = _attn_decode(q, kb, vb, cache_k.reshape(depth * n_pool, page * n_heads, V7X_LANES),
                           cache_v.reshape(depth * n_pool, page * n_heads, DA_V), page_table, l * n_pool,
                           *lam_vecs, n_heads=n_heads, gp=16, lam_init=lam_init)
        r_, k2_, v_, na_, b_, lw_, gate, bonus = _rwkv_prep_tok(
            prw, state_shift[l].reshape(db, -1), rw_weights, ones_bd)
        orw, wkv_s = _rwkv_step(state_wkv[l], r_, k2_, v_, na_, b_, lw_)
        h2 = _outproj_ln(oda, orw, bonus, gate, h1, wo, *lnx, ones_bd, ln[2], ln[3],
                         alpha=alpha, tm=db)
        y_s = _ffn_ln(h2, wb_gu, wb_dn, ln[4], ln[5], alpha=alpha, tm=db, tf=tf)
        outs[4].append(k.reshape(db, 1, n_heads, 2 * DA_QK))
        outs[5].append(v.reshape(db, 1, n_heads, DA_V))
        outs[6].append(wkv_s)
        outs[7].append(prw.reshape(db, 1, -1))

    stacked = [jnp.stack(o) for o in outs]
    return (y_p.reshape(bsz, seq, d_model), y_s.reshape(db, 1, d_model), *stacked)
```

```python
import functools
import math

import jax
import jax.numpy as jnp
import numpy as np
from jax import lax
from jax.experimental import pallas as pl
from jax.experimental.pallas import tpu as pltpu

F32 = jnp.float32
BF16 = jnp.bfloat16

DA_QK = 64
DA_V = 2 * DA_QK
RW_HEAD = 64
RW_W_LORA = 64
RW_A_LORA = 64
RW_G_LORA = 128
RW_GN_EPS = 64e-5
ROPE_THETA = 10000.0
LN_EPS = 1e-5
NEG = -1e30
LOG2E = math.log2(math.e)

V7X_LANES = 128
V7X_VMEM_LIMIT_BYTES = 56 * 1024 * 1024

RW_CHUNK = 64
RW_GROUP = 4
RW_NG = RW_CHUNK * RW_GROUP


def _cparams(semantics):
    return pltpu.CompilerParams(dimension_semantics=semantics,
                                vmem_limit_bytes=V7X_VMEM_LIMIT_BYTES)


def _dot(a, b):
    return jnp.dot(a.astype(BF16), b.astype(BF16), preferred_element_type=F32)


def _dot_nt(a, b):
    return lax.dot_general(a.astype(BF16), b.astype(BF16), (((1,), (1,)), ((), ())),
                           preferred_element_type=F32)


def _split(x):
    hi = x.astype(BF16)
    lo = (x - hi.astype(F32)).astype(BF16)
    return hi, lo


def _dot_exact_rhs(a, b):
    ah, al = _split(a)
    return (jnp.dot(ah, b, preferred_element_type=F32)
            + jnp.dot(al, b, preferred_element_type=F32))


def _layernorm(y, g, b):
    mu = jnp.mean(y, axis=-1, keepdims=True)
    d = y - mu
    var = jnp.mean(d * d, axis=-1, keepdims=True)
    return d * lax.rsqrt(var + LN_EPS) * g + b


def _ffn_ln_kernel(x_ref, wgu_ref, wd_ref, g_ref, b_ref, o_ref, *, d_ff, tf, alpha):
    x = x_ref[...]
    xb = x.astype(BF16)

    def activation(c):
        gate = jnp.dot(xb, wgu_ref[:, c * tf:(c + 1) * tf], preferred_element_type=F32)
        up = jnp.dot(xb, wgu_ref[:, d_ff + c * tf:d_ff + (c + 1) * tf], preferred_element_type=F32)
        return (gate * jax.nn.sigmoid(gate) * up).astype(BF16)

    nf = d_ff // tf
    act = activation(0)
    acc = None
    for c in range(nf):
        nxt = activation(c + 1) if c + 1 < nf else None
        part = jnp.dot(act, wd_ref[c * tf:(c + 1) * tf, :], preferred_element_type=F32)
        acc = part if acc is None else acc + part
        act = nxt
    o_ref[...] = _layernorm(alpha * x + 0.5 * acc, g_ref[...], b_ref[...])


def _ffn_ln(x, w_gu, w_down, g, b, *, alpha, tm, tf):
    m, d = x.shape
    f = w_down.shape[0]
    assert m % tm == 0 and f % tf == 0
    const = lambda shape: pl.BlockSpec(shape, lambda i: (0, 0))
    return pl.pallas_call(
        functools.partial(_ffn_ln_kernel, d_ff=f, tf=tf, alpha=alpha),
        grid=(m // tm,),
        in_specs=[pl.BlockSpec((tm, d), lambda i: (i, 0)), const(w_gu.shape), const(w_down.shape),
                  const((1, d)), const((1, d))],
        out_specs=pl.BlockSpec((tm, d), lambda i: (i, 0)),
        out_shape=jax.ShapeDtypeStruct((m, d), F32),
        compiler_params=_cparams(("parallel",)),
        name="ffn_ln",
    )(x, w_gu, w_down, g, b)


def _inproj_kernel(h_ref, w_ref, cos_ref, sin_ref, prev_ref, mu_ref, w0_ref, w2p_ref, a0_ref,
                   a2p_ref, g2_ref, kk_ref, ka_ref, rk_ref, ones_ref,
                   q_ref, k_ref, kb_ref, v_ref, vb_ref, shift_ref, *rest,
                   n_heads, scale, steps_per_seq):
    rw_outs, carry_ref, xs_ref = rest[:-2], rest[-2], rest[-1]
    hb = h_ref[...].astype(BF16)
    tm = hb.shape[0]
    cos = cos_ref[...]
    sin = sin_ref[...]
    lane = lax.broadcasted_iota(jnp.int32, cos.shape, 1)
    first_half = (lane % DA_QK) < (DA_QK // 2)
    qc = n_heads * 2 * DA_QK

    def rope_head(x):
        rot = jnp.where(first_half, pltpu.roll(x, V7X_LANES - DA_QK // 2, 1),
                        pltpu.roll(x, DA_QK // 2, 1))
        return x * cos + rot * sin

    def project_q():
        q = jnp.dot(hb, w_ref[:, 0:qc], preferred_element_type=F32)
        for h in range(n_heads):
            sl = slice(h * V7X_LANES, (h + 1) * V7X_LANES)
            q_ref[:, sl] = (rope_head(q[:, sl]) * scale).astype(BF16)

    def project_kv():
        k = jnp.dot(hb, w_ref[:, qc:2 * qc], preferred_element_type=F32)
        v = jnp.dot(hb, w_ref[:, 2 * qc:2 * qc + vc], preferred_element_type=F32)
        for h in range(n_heads):
            kh = rope_head(k[:, h * V7X_LANES:(h + 1) * V7X_LANES])
            kb_ref[:, h * V7X_LANES:(h + 1) * V7X_LANES] = kh.astype(BF16)
            k_ref[pl.ds(h, tm, stride=n_heads), :] = kh
            v_ref[pl.ds(h, tm, stride=n_heads), :] = v[:, h * DA_V:(h + 1) * DA_V]
        vb_ref[...] = v.astype(BF16)

    vc = n_heads * DA_V
    def project_rw():
        return jnp.dot(hb, w_ref[:, 2 * qc + vc:], preferred_element_type=F32)

    if steps_per_seq is None:
        x = project_rw()
        prev = prev_ref[...]
        shift_ref[...] = x
        before_dots, after_dots = project_q, project_kv
    else:
        i = pl.program_id(0)

        @pl.when(i == 0)
        def _():
            xs_ref[...] = jnp.zeros(xs_ref.shape, F32)
            carry_ref[...] = jnp.zeros(carry_ref.shape, F32)

        @pl.when((i - 1) % steps_per_seq == 0)
        def _():
            carry_ref[...] = prev_ref[0]

        x = xs_ref[...]
        row = lax.broadcasted_iota(jnp.int32, x.shape, 0)
        prev = jnp.where(row == 0, carry_ref[...], pltpu.roll(x, 1, 0))
        carry_ref[...] = x[tm - 1:tm, :]
        shift_ref[0] = x[tm - 1:tm, :]
        parked = []

        before_dots = None

        def after_dots():
            parked.append(project_rw())
            project_q()
            project_kv()

    vals = _rwkv_prep_math(x, prev, mu_ref[...], w0_ref[...], w2p_ref[...], a0_ref[...],
                           a2p_ref[...], g2_ref[...], kk_ref[...], ka_ref[...], rk_ref[...],
                           ones_ref[...], before_dots=before_dots, after_dots=after_dots)
    for o_ref, val in zip(rw_outs, vals):
        o_ref[...] = val.astype(o_ref.dtype)
    if steps_per_seq is not None:
        xs_ref[...] = parked[0]


def _inproj(h, w_in, cos_tab, sin_tab, prev, rw_consts, *, n_heads, tm, seq_len, rw_dtype):
    m, d = h.shape
    cols = w_in.shape[1]
    qc = n_heads * 2 * DA_QK
    vc = n_heads * DA_V
    rw = cols - 2 * qc - vc
    rd = rw_consts[-1].shape[0]
    n_pos_blocks = cos_tab.shape[0] // tm
    n = m // tm
    if seq_len is None:
        steps_per_seq = None
        n_steps = n
        tok = lambda i: (i, 0)
        lag = tok
        prev_spec = pl.BlockSpec((tm, rw), tok)
        shift_spec, shift_shape = pl.BlockSpec((tm, rw), tok), (m, rw)
    else:
        assert seq_len % tm == 0
        steps_per_seq = seq_len // tm
        n_steps = n + 1
        tok = lambda i: (jnp.minimum(i, n - 1), 0)
        lag = lambda i: (jnp.maximum(i - 1, 0), 0)
        prev_spec = pl.BlockSpec((1, 1, rw),
                                 lambda i: (jnp.maximum(i - 1, 0) // steps_per_seq, 0, 0))
        shift_spec, shift_shape = prev_spec, (m // seq_len, 1, rw)
    pos = lambda i: (tok(i)[0] % n_pos_blocks, 0)
    rw_dtypes = [F32 if i == 5 else rw_dtype for i in range(8)]
    return pl.pallas_call(
        functools.partial(_inproj_kernel, n_heads=n_heads, scale=DA_QK ** -0.5 * LOG2E,
                          steps_per_seq=steps_per_seq),
        grid=(n_steps,),
        in_specs=[
            pl.BlockSpec((tm, d), tok),
            pl.BlockSpec((d, cols), lambda i: (0, 0)),
            pl.BlockSpec((tm, V7X_LANES), pos),
            pl.BlockSpec((tm, V7X_LANES), pos),
            prev_spec,
        ] + _weight_specs(rw_consts, 1),
        out_specs=[
            pl.BlockSpec((tm, qc), tok), pl.BlockSpec((tm * n_heads, 2 * DA_QK), tok),
            pl.BlockSpec((tm, qc), tok), pl.BlockSpec((tm * n_heads, DA_V), tok),
            pl.BlockSpec((tm, vc), tok), shift_spec,
        ] + [pl.BlockSpec((tm, rd), lag)] * 8,
        out_shape=[
            jax.ShapeDtypeStruct((m, qc), BF16), jax.ShapeDtypeStruct((m * n_heads, 2 * DA_QK), F32),
            jax.ShapeDtypeStruct((m, qc), BF16), jax.ShapeDtypeStruct((m * n_heads, DA_V), F32),
            jax.ShapeDtypeStruct((m, vc), BF16), jax.ShapeDtypeStruct(shift_shape, F32),
        ] + [jax.ShapeDtypeStruct((m, rd), dt) for dt in rw_dtypes],
        scratch_shapes=[pltpu.VMEM((1, rw), F32), pltpu.VMEM((tm, rw), F32)],
        compiler_params=_cparams(("arbitrary",)),
        name="inproj_rope",
    )(h, w_in, cos_tab, sin_tab, prev, *rw_consts)


def _rope_tables(positions):
    half = DA_QK // 2
    inv = ROPE_THETA ** (-jnp.arange(half, dtype=F32) / half)
    ang = positions.astype(F32)[:, None] * inv[None, :]
    cos = jnp.cos(ang)
    sin = jnp.sin(ang)
    cos_tab = jnp.concatenate([cos, cos, cos, cos], axis=1)
    sin_tab = jnp.concatenate([-sin, sin, -sin, sin], axis=1)
    return cos_tab, sin_tab


def _lambda(lq1_ref, lk1_ref, lq2_ref, lk2_ref, lam_init):
    s1 = jnp.sum(lq1_ref[...] * lk1_ref[...], axis=-1, keepdims=True)
    s2 = jnp.sum(lq2_ref[...] * lk2_ref[...], axis=-1, keepdims=True)
    return jnp.exp(s1) - jnp.exp(s2) + lam_init


def _subln(o, g, lam_init):
    return o * lax.rsqrt(jnp.mean(o * o, axis=-1, keepdims=True) + LN_EPS) * g * (1.0 - lam_init)


def _attn_prompt_kernel(lq1_ref, lk1_ref, lq2_ref, lk2_ref, g_ref, q_ref, k_ref, v_ref, o_ref,
                        qm_ref, vx_ref, s_ref, p_ref, m_ref, acc_ref, *, tq, rs, lam_init):
    i = pl.program_id(2)

    @pl.when(i == 0)
    def _():
        vx_ref[:, 0:DA_V] = v_ref[0]
        vx_ref[:, DA_V:] = jnp.ones((vx_ref.shape[0], V7X_LANES), BF16)

    q = q_ref[0].astype(F32)
    lane = lax.broadcasted_iota(jnp.int32, q.shape, 1)
    qm_ref[0] = jnp.where(lane < DA_QK, q, 0.0).astype(BF16)
    qm_ref[1] = jnp.where(lane >= DA_QK, q, 0.0).astype(BF16)
    m_ref[...] = jnp.full(m_ref.shape, NEG, F32)
    acc_ref[...] = jnp.zeros(acc_ref.shape, F32)
    nt = tq // V7X_LANES

    kw = 2 * V7X_LANES

    def score_parts(j, buf):
        def part(mp, t):
            kblk = k_ref[0, pl.ds(pl.multiple_of(j * tq + t * kw, kw), kw), :]
            s_ref[buf, mp, :, t * kw:(t + 1) * kw] = lax.dot_general(
                qm_ref[mp], kblk, (((1,), (1,)), ((), ())), preferred_element_type=F32)
        return [functools.partial(part, mp, t) for t in range(tq // kw) for mp in range(2)]

    def scores(j, buf):
        for part in score_parts(j, buf):
            part()

    def softmax_pv(j, buf, diagonal, fillers=()):
        off = pl.multiple_of(j * tq, tq)
        fillers = list(fillers)
        for r in range(tq // rs):
            if fillers:
                fillers.pop(0)()
            rows = slice(r * rs, (r + 1) * rs)
            n_vis = min(nt, pl.cdiv((r + 1) * rs, V7X_LANES)) if diagonal else nt
            vblk = vx_ref[pl.ds(off, n_vis * V7X_LANES), :]
            def tile(mp, t):
                s = s_ref[buf, mp, rows, t * V7X_LANES:(t + 1) * V7X_LANES]
                if diagonal and (t + 1) * V7X_LANES > r * rs + 1:
                    row = lax.broadcasted_iota(jnp.int32, s.shape, 0) + r * rs
                    col = lax.broadcasted_iota(jnp.int32, s.shape, 1) + t * V7X_LANES
                    s = jnp.where(col <= row, s, NEG)
                return s

            for mp in range(2):
                mx = functools.reduce(jnp.maximum, [tile(mp, t) for t in range(n_vis)])
                m_old = m_ref[mp, rows, :]
                m_new = jnp.maximum(m_old, jnp.max(mx, axis=1, keepdims=True))
                for t in range(n_vis):
                    p_ref[mp, rows, t * V7X_LANES:(t + 1) * V7X_LANES] = jnp.exp2(
                        tile(mp, t) - m_new).astype(BF16)
                c = jnp.exp2(m_old - m_new)
                m_ref[mp, rows, :] = m_new
                pv = jnp.dot(p_ref[mp, rows, 0:n_vis * V7X_LANES], vblk, preferred_element_type=F32)
                acc_ref[mp, rows, :] = jnp.concatenate([c, c], axis=1) * acc_ref[mp, rows, :] + pv

    scores(0, 0)

    def body(jj, carry):
        j = 2 * jj
        softmax_pv(j, 0, False, score_parts(j + 1, 1))
        softmax_pv(j + 1, 1, False, score_parts(j + 2, 0))
        return carry

    lax.fori_loop(0, i // 2, body, 0)

    @pl.when(i % 2 == 0)
    def _():
        softmax_pv(i, 0, True)

    @pl.when(i % 2 == 1)
    def _():
        softmax_pv(i - 1, 0, False, score_parts(i, 1))
        softmax_pv(i, 1, True)

    lam = _lambda(lq1_ref, lk1_ref, lq2_ref, lk2_ref, lam_init)
    o = (acc_ref[0, :, 0:DA_V] / acc_ref[0, :, DA_V:]
         - lam * (acc_ref[1, :, 0:DA_V] / acc_ref[1, :, DA_V:]))
    o_ref[0] = _subln(o, g_ref[...], lam_init).astype(o_ref.dtype)


def _attn_prompt(q, kb, vb, lq1, lk1, lq2, lk2, subln_g, *, n_heads, tq, rs, lam_init):
    b, s, _ = q.shape
    vec = lambda n: pl.BlockSpec((1, n), lambda bi, h, i: (0, 0))
    return pl.pallas_call(
        functools.partial(_attn_prompt_kernel, tq=tq, rs=rs, lam_init=lam_init),
        grid=(b, n_heads, s // tq),
        in_specs=[
            vec(DA_QK), vec(DA_QK), vec(DA_QK), vec(DA_QK), vec(DA_V),
            pl.BlockSpec((1, tq, V7X_LANES), lambda bi, h, i: (bi, i, h)),
            pl.BlockSpec((1, s, V7X_LANES), lambda bi, h, i: (bi, 0, h)),
            pl.BlockSpec((1, s, DA_V), lambda bi, h, i: (bi, 0, h)),
        ],
        out_specs=pl.BlockSpec((1, tq, DA_V), lambda bi, h, i: (bi, i, h)),
        out_shape=jax.ShapeDtypeStruct((b, s, n_heads * DA_V), BF16),
        scratch_shapes=[pltpu.VMEM((2, tq, V7X_LANES), BF16), pltpu.VMEM((s, DA_V + V7X_LANES), BF16),
                        pltpu.VMEM((2, 2, tq, tq), F32), pltpu.VMEM((2, tq, tq), BF16),
                        pltpu.VMEM((2, tq, V7X_LANES), F32),
                        pltpu.VMEM((2, tq, DA_V + V7X_LANES), F32)],
        compiler_params=_cparams(("parallel", "parallel", "arbitrary")),
        name="attn_prompt",
    )(lq1, lk1, lq2, lk2, subln_g, q, kb, vb)


def _attn_decode_kernel(pt_ref, lq1_ref, lk1_ref, lq2_ref, lk2_ref, g_ref, q_ref, kn_ref, vn_ref,
                        *rest, n_heads, n_steps, gp, lam_init):
    del pt_ref
    kp_refs, vp_refs = rest[:gp], rest[gp:2 * gp]
    o_ref, qm_ref, ex_ref, m_ref, l_ref, acc_ref = rest[2 * gp:]
    p = pl.program_id(1)
    nj = 2 * n_heads
    dn = (((1,), (1,)), ((), ()))
    sub = lax.broadcasted_iota(jnp.int32, (8, V7X_LANES), 0)
    lane = lax.broadcasted_iota(jnp.int32, (8, V7X_LANES), 1)
    live = (sub % n_heads) == (lane // 2)

    def head_rows(x):
        out = jnp.zeros((8, V7X_LANES), F32)
        for h in range(n_heads):
            xh = jnp.broadcast_to(x[:, h * V7X_LANES:(h + 1) * V7X_LANES], (8, V7X_LANES))
            out = jnp.where(sub == h, xh, out)
        return out

    def per_row(x, mp):
        pick = lane == 2 * (sub % n_heads) + mp
        return jnp.broadcast_to(jnp.sum(jnp.where(pick, x, 0.0), axis=1, keepdims=True),
                                (8, V7X_LANES))

    def softmax_rows(s, rows_live):
        r = s.shape[0]
        s3 = jnp.where(rows_live, s.reshape(r // 8, 8, V7X_LANES), NEG)
        m_g = jnp.max(jnp.max(s3, axis=0), axis=0, keepdims=True)
        pr = jnp.exp2(s3 - m_g).reshape(r, V7X_LANES)
        return m_g, pr, jnp.sum(pr, axis=0, keepdims=True)

    def weighted_values(pb, v):
        r = v.shape[0]
        return [jnp.sum((pb[:, mp * V7X_LANES:(mp + 1) * V7X_LANES] * v)
                        .reshape(r // 8, 8, V7X_LANES), axis=0) for mp in range(2)]

    @pl.when(p == 0)
    def _():
        e_r = lax.broadcasted_iota(jnp.int32, ex_ref.shape, 0)
        e_c = lax.broadcasted_iota(jnp.int32, ex_ref.shape, 1)
        ex_ref[...] = jnp.where((e_r < nj) & (e_r % 2 == e_c // V7X_LANES), 1.0, 0.0).astype(BF16)
        qrows = jnp.zeros((8, V7X_LANES), F32)
        q = q_ref[0].astype(F32)
        for h in range(n_heads):
            qh = jnp.broadcast_to(q[:, h * V7X_LANES:(h + 1) * V7X_LANES], (8, V7X_LANES))
            qrows = jnp.where(sub // 2 == h, qh, qrows)
        qrows = jnp.where(lane // DA_QK == sub % 2, qrows, 0.0)
        qm_ref[...] = jnp.zeros(qm_ref.shape, BF16)
        qm_ref[0:8, :] = qrows.astype(BF16)
        kn = head_rows(kn_ref[0].astype(F32))
        s_new = lax.dot_general(kn.astype(BF16), qm_ref[...], dn, preferred_element_type=F32)
        m_g, pr, l_g = softmax_rows(s_new, live & (sub < n_heads))
        m_ref[...] = m_g
        l_ref[...] = l_g
        pb = jnp.dot(pr.astype(BF16), ex_ref[...], preferred_element_type=F32)
        parts = weighted_values(pb, head_rows(vn_ref[0].astype(F32)))
        for mp in range(2):
            acc_ref[mp] = parts[mp]

    qm = qm_ref[...]
    s_pg = [lax.dot_general(kp_refs[g][0].astype(BF16), qm, dn, preferred_element_type=F32)
            for g in range(gp)]
    stats = [softmax_rows(s, live) for s in s_pg]
    ex = ex_ref[...]
    pb_pg = [jnp.dot(st[1].astype(BF16), ex, preferred_element_type=F32) for st in stats]
    part_pg = [weighted_values(pb, vp_refs[g][0]) for g, pb in enumerate(pb_pg)]
    m_old = m_ref[...]
    m_new = functools.reduce(jnp.maximum, [st[0] for st in stats], m_old)
    c = jnp.exp2(m_old - m_new)
    w_pg = [jnp.exp2(st[0] - m_new) for st in stats]
    l_ref[...] = c * l_ref[...] + functools.reduce(
        jnp.add, [w * st[2] for w, st in zip(w_pg, stats)])
    m_ref[...] = m_new
    for mp in range(2):
        acc = per_row(c, mp) * acc_ref[mp]
        for g in range(gp):
            acc = acc + per_row(w_pg[g], mp) * part_pg[g][mp]
        acc_ref[mp] = acc

    @pl.when(p == n_steps - 1)
    def _():
        lam = _lambda(lq1_ref, lk1_ref, lq2_ref, lk2_ref, lam_init)
        on = [acc_ref[mp] / per_row(l_ref[...], mp) for mp in range(2)]
        for h in range(n_heads):
            o1, o2 = [functools.reduce(jnp.add, [x[i:i + 1, :] for i in range(h, 8, n_heads)])
                      for x in on]
            o_ref[0, :, h * V7X_LANES:(h + 1) * V7X_LANES] = _subln(
                o1 - lam * o2, g_ref[...], lam_init).astype(o_ref.dtype)


def _attn_decode(q, kb, vb, cache_k, cache_v, page_table, page_base, lq1, lk1, lq2, lk2, subln_g,
                 *, n_heads, gp, lam_init):
    db, width = q.shape
    n_pages = page_table.shape[1]
    rows = cache_k.shape[1]
    assert 8 % n_heads == 0 and rows % 8 == 0
    n_steps = n_pages // gp
    vec = lambda n: pl.BlockSpec((1, n), lambda bi, p, pt: (0, 0))
    tok = pl.BlockSpec((1, 1, width), lambda bi, p, pt: (bi, 0, 0))
    pages = [pl.BlockSpec((1, rows, V7X_LANES),
                          lambda bi, p, pt, g=g: (page_base + pt[bi, p * gp + g], 0, 0))
             for g in range(gp)]
    grid_spec = pltpu.PrefetchScalarGridSpec(
        num_scalar_prefetch=1,
        grid=(db, n_steps),
        in_specs=[vec(DA_QK), vec(DA_QK), vec(DA_QK), vec(DA_QK), vec(DA_V), tok, tok, tok]
                 + pages + pages,
        out_specs=tok,
        scratch_shapes=[pltpu.VMEM((V7X_LANES, V7X_LANES), BF16),
                        pltpu.VMEM((V7X_LANES, 2 * V7X_LANES), BF16),
                        pltpu.VMEM((1, V7X_LANES), F32), pltpu.VMEM((1, V7X_LANES), F32),
                        pltpu.VMEM((2, 8, V7X_LANES), F32)],
    )
    out = pl.pallas_call(
        functools.partial(_attn_decode_kernel, n_heads=n_heads, n_steps=n_steps, gp=gp,
                          lam_init=lam_init),
        grid_spec=grid_spec,
        out_shape=jax.ShapeDtypeStruct((db, 1, width), BF16),
        compiler_params=_cparams(("parallel", "arbitrary")),
        name="attn_decode",
    )(page_table, lq1, lk1, lq2, lk2, subln_g,
      q.reshape(db, 1, width), kb.reshape(db, 1, width), vb.reshape(db, 1, width),
      *([cache_k] * gp), *([cache_v] * gp))
    return out.reshape(db, width)


def _rwkv_prep_math(x, prev, mu, w0, w2p, a0, a2p, g2, k_k, k_a, r_k, ones_bd,
                    before_dots=None, after_dots=None):
    rd = ones_bd.shape[0]
    xm = x + (prev - x) * mu
    r = xm[:, 0:rd]
    k = xm[:, rd:2 * rd]
    v = xm[:, 2 * rd:3 * rd]
    xwa = xm[:, 3 * rd:3 * rd + RW_W_LORA + RW_A_LORA]
    xg = xm[:, 3 * rd + RW_W_LORA + RW_A_LORA:]
    tw = jnp.tanh(xwa)
    sg = jax.nn.sigmoid(xg)
    kk = k * k_k
    kk_sq = kk * kk
    if before_dots is not None:
        before_dots()
    lora_w = _dot(tw, w2p)
    lora_a = _dot(xwa, a2p)
    g = _dot(sg, g2)
    n2 = _dot_exact_rhs(kk_sq, ones_bd)
    if after_dots is not None:
        after_dots()
    zw = -(w0 + lora_w)
    softplus = jnp.maximum(zw, 0.0) + jnp.log(1.0 + jnp.exp(-jnp.abs(zw)))
    logw = -jnp.exp(-softplus - 0.5)
    a = jax.nn.sigmoid(a0 + lora_a)
    kk = kk / jnp.maximum(jnp.sqrt(n2), 1e-12)
    k2 = k * (1.0 + (a - 1.0) * k_a)
    bonus = _dot_exact_rhs(r * k2 * r_k, ones_bd) * v
    return r, k2, v, -kk, kk * a, logw, g, bonus


def _rwkv_weights(p, rd):
    zeros_w = jnp.zeros((RW_A_LORA, rd), F32)
    zeros_a = jnp.zeros((RW_W_LORA, rd), F32)
    return [p['rw_mu'][None, :], p['rw_w0'][None, :],
            jnp.concatenate([p['rw_w2'], zeros_w], axis=0).astype(BF16), p['rw_a0'][None, :],
            jnp.concatenate([zeros_a, p['rw_a2']], axis=0).astype(BF16), p['rw_g2'].astype(BF16),
            p['rw_k_k'][None, :], p['rw_k_a'][None, :], p['rw_r_k'].reshape(1, rd)]


def _weight_specs(weights, nargs):
    zero = {1: lambda i: (0, 0), 2: lambda i, j: (0, 0)}[nargs]
    return [pl.BlockSpec(w.shape, zero) for w in weights]


def _rwkv_masks():
    n = RW_NG
    row = np.arange(n)[:, None]
    col = np.arange(n)[None, :]
    same = (row // RW_CHUNK) == (col // RW_CHUNK)
    masks = [same & (col < row), same & (col <= row), (row // 2 == col // 2) & (col < row)]
    s = 2
    while s < RW_CHUNK:
        masks.append(((row // s) % 2 == 1) & ((col // s) == (row // s) - 1))
        s *= 2
    return np.stack(masks).astype(np.float32)


def _rwkv_chunk_chains(chains, masks_ref, tri, s_olds):
    c, ng = chains[0][0].shape
    lane = lax.broadcasted_iota(jnp.int32, (c, ng), 1)
    row = lax.broadcasted_iota(jnp.int32, (ng, ng), 0)
    col = lax.broadcasted_iota(jnp.int32, (ng, ng), 1)
    d = functools.partial(jnp.dot, preferred_element_type=F32)
    cat0 = lambda *xs: jnp.concatenate(xs, axis=0)
    cat1 = lambda *xs: jnp.concatenate(xs, axis=1)

    def stack(x):
        return cat0(*[jnp.where(lane // RW_HEAD == h, x, 0.0) for h in range(ng // RW_HEAD)])

    def split3(logw):
        hi = logw.astype(BF16)
        rem = logw - hi.astype(F32)
        mid = rem.astype(BF16)
        return hi, mid, (rem - mid.astype(F32)).astype(BF16)

    parts = [split3(ch[5]) for ch in chains]
    cums = [d(tri, hi) + d(tri, mid) + d(tri, lo) for hi, mid, lo in parts]

    def decayed(ch, cum):
        r, k2, v, na, b, logw = ch
        cum_c = cum[c - 1:c, :]
        g_inv = jnp.exp(-cum)
        g_tail = jnp.exp(cum_c - cum)
        return dict(at=stack(na * jnp.exp(cum - logw)), rt=stack(r * jnp.exp(cum)),
                    btkt=cat0(stack(b * g_inv), stack(k2 * g_inv)),
                    bhkh=cat0(stack(b * g_tail), stack(k2 * g_tail)), vm=stack(v),
                    gc=jnp.exp(cum_c))

    q = [decayed(ch, cum) for ch, cum in zip(chains, cums)]
    grams = [_dot_nt(cat0(x['at'], x['rt']), x['btkt']) for x in q]
    strict = masks_ref[0]
    incl = masks_ref[1]
    ns = [g[:ng, :ng] * strict for g in grams]
    gaks = [g[:ng, ng:] * strict for g in grams]
    grbs = [g[ng:, :ng] * incl for g in grams]
    grks = [g[ng:, ng:] * incl for g in grams]

    eye = jnp.where(row == col, 1.0, 0.0)
    xs = [eye + n * masks_ref[2] for n in ns]
    for lvl in range(3, masks_ref.shape[0]):
        lvl_mask = masks_ref[lvl]
        ys = [_dot(x, n * lvl_mask) for x, n in zip(xs, ns)]
        xs = [x + _dot(y, x) for x, y in zip(xs, ys)]

    gvs = [_dot(gak, x['vm']) for gak, x in zip(gaks, q)]
    aus = [_dot(x, cat1(y['at'], gv)) for x, y, gv in zip(xs, q, gvs)]
    ts = [_dot(grb, au) for grb, au in zip(grbs, aus)]
    gkvs = [_dot(grk, x['vm']) for grk, x in zip(grks, q)]
    rbs = [x['rt'] + t[:, :ng] for x, t in zip(q, ts)]
    obs = [t[:, ng:] + gkv for t, gkv in zip(ts, gkvs)]
    ms = [jnp.where(row == col, x['gc'], 0.0) + _dot(au[:, :ng].T, x['bhkh'][:ng])
          for x, au in zip(q, aus)]
    zs = [_dot(cat0(au[:, ng:], x['vm']).T, x['bhkh']) for x, au in zip(q, aus)]

    os_ = [_dot_nt(rb, s_old) + ob for rb, s_old, ob in zip(rbs, s_olds, obs)]
    s_news = [_dot(s_old, m) + z for s_old, m, z in zip(s_olds, ms, zs)]
    o_toks = [functools.reduce(jnp.add, [o[h * c:(h + 1) * c] for h in range(ng // RW_HEAD)])
              for o in os_]
    return o_toks, s_news


def _rwkv_chunk_kernel(r_ref, k_ref, v_ref, na_ref, b_ref, lw_ref, masks_ref, tri_ref,
                       o_ref, sout_ref, s_ref, *, bb, n_groups, n_chunks):
    ci = pl.program_id(1)

    @pl.when(ci == 0)
    def _():
        s_ref[...] = jnp.zeros_like(s_ref)

    ids = [(bi, gi) for bi in range(bb) for gi in range(n_groups)]
    sl = lambda gi: slice(gi * RW_NG, (gi + 1) * RW_NG)
    chains = [tuple(ref[bi, :, sl(gi)].astype(F32)
                    for ref in (r_ref, k_ref, v_ref, na_ref, b_ref, lw_ref))
              for bi, gi in ids]
    o_toks, s_news = _rwkv_chunk_chains(chains, masks_ref, tri_ref[...],
                                        [s_ref[bi * n_groups + gi] for bi, gi in ids])
    for (bi, gi), o_tok, s_new in zip(ids, o_toks, s_news):
        o_ref[bi, :, sl(gi)] = o_tok
        s_ref[bi * n_groups + gi] = s_new

    @pl.when(ci == n_chunks - 1)
    def _():
        for bi, gi in ids:
            s = s_ref[bi * n_groups + gi]
            for h in range(RW_GROUP):
                hs = slice(h * RW_HEAD, (h + 1) * RW_HEAD)
                sout_ref[bi, gi * RW_GROUP + h] = s[hs, hs]


def _rwkv_chunked(r, k2, v, na, b, logw, *, bb):
    bsz, s, rd = r.shape
    n_groups = rd // RW_NG
    n_chunks = s // RW_CHUNK
    n_heads = rd // RW_HEAD
    masks = jnp.asarray(_rwkv_masks())
    tri = jnp.asarray(np.tril(np.ones((RW_CHUNK, RW_CHUNK), np.float32))).astype(BF16)
    tok = pl.BlockSpec((bb, RW_CHUNK, rd), lambda bi, ci: (bi, ci, 0))
    return pl.pallas_call(
        functools.partial(_rwkv_chunk_kernel, bb=bb, n_groups=n_groups, n_chunks=n_chunks),
        grid=(bsz // bb, n_chunks),
        in_specs=[tok] * 6 + [pl.BlockSpec(masks.shape, lambda bi, ci: (0, 0, 0)),
                              pl.BlockSpec(tri.shape, lambda bi, ci: (0, 0))],
        out_specs=[tok, pl.BlockSpec((bb, n_heads, RW_HEAD, RW_HEAD), lambda bi, ci: (bi, 0, 0, 0))],
        out_shape=[jax.ShapeDtypeStruct((bsz, s, rd), F32),
                   jax.ShapeDtypeStruct((bsz, n_heads, RW_HEAD, RW_HEAD), F32)],
        scratch_shapes=[pltpu.VMEM((bb * n_groups, RW_NG, RW_NG), F32)],
        compiler_params=_cparams(("parallel", "arbitrary")),
        name="rwkv_chunked",
    )(r, k2, v, na, b, logw, masks, tri)


def _rwkv_step_kernel(s_ref, r_ref, k_ref, na_ref, b_ref, lw_ref, v_ref, sout_ref, o_ref):
    s = s_ref[0]
    sa = jnp.sum(s * na_ref[0], axis=-1, keepdims=True)
    s_new = s * jnp.exp(lw_ref[0]) + sa * b_ref[0] + v_ref[0] * k_ref[0]
    sout_ref[0] = s_new
    o_ref[0] = jnp.sum(s_new * r_ref[0], axis=-1, keepdims=True)


def _rwkv_step(state, r, k2, v, na, b, logw):
    db, nh, hd, _ = state.shape
    rowv = lambda x: x.reshape(db, nh, 1, hd)
    rspec = pl.BlockSpec((1, nh, 1, hd), lambda i: (i, 0, 0, 0))
    cspec = pl.BlockSpec((1, nh, hd, 1), lambda i: (i, 0, 0, 0))
    sspec = pl.BlockSpec((1, nh, hd, hd), lambda i: (i, 0, 0, 0))
    s_new, o = pl.pallas_call(
        _rwkv_step_kernel,
        grid=(db,),
        in_specs=[sspec, rspec, rspec, rspec, rspec, rspec, cspec],
        out_specs=[sspec, cspec],
        out_shape=[jax.ShapeDtypeStruct(state.shape, F32), jax.ShapeDtypeStruct((db, nh, hd, 1), F32)],
        compiler_params=_cparams(("parallel",)),
        name="rwkv_step",
    )(state, rowv(r), rowv(k2), rowv(na), rowv(b), rowv(logw), v.reshape(db, nh, hd, 1))
    return o.reshape(db, nh * hd), s_new


def _outproj_ln_kernel(oda_ref, orw_ref, bonus_ref, gate_ref, h_ref, w_ref, lg_ref, lb_ref,
                       ones_ref, g_ref, b_ref, o_ref, *, alpha):
    o = orw_ref[...]
    ones_bd = ones_ref[...]
    inv_n = 1.0 / RW_HEAD
    mu = _dot_exact_rhs(o, ones_bd) * inv_n
    d = o - mu
    var = _dot_exact_rhs(d * d, ones_bd) * inv_n
    o = d * lax.rsqrt(var + RW_GN_EPS) * lg_ref[...] + lb_ref[...]
    o_rw = (o + bonus_ref[...]) * gate_ref[...]
    da = oda_ref.shape[1]
    mix = (jnp.dot(oda_ref[...], w_ref[0:da, :], preferred_element_type=F32)
           + jnp.dot(o_rw.astype(BF16), w_ref[da:, :], preferred_element_type=F32))
    o_ref[...] = _layernorm(alpha * h_ref[...] + mix, g_ref[...], b_ref[...])


def _outproj_ln(oda, orw, bonus, gate, h, w_out, lnx_g, lnx_b, ones_bd, g, b, *, alpha, tm):
    m, d = h.shape
    da = oda.shape[1]
    rd = orw.shape[1]
    tok = lambda n: pl.BlockSpec((tm, n), lambda i: (i, 0))
    consts = [w_out, lnx_g, lnx_b, ones_bd, g, b]
    return pl.pallas_call(
        functools.partial(_outproj_ln_kernel, alpha=alpha),
        grid=(m // tm,),
        in_specs=[tok(da), tok(rd), tok(rd), tok(rd), tok(d)] + _weight_specs(consts, 1),
        out_specs=tok(d),
        out_shape=jax.ShapeDtypeStruct((m, d), F32),
        compiler_params=_cparams(("parallel",)),
        name="outproj_ln",
    )(oda, orw, bonus, gate, h, *consts)


def _tile(m, pref):
    return pref if m % pref == 0 else m


def kernel(x_prompt, x_sample, cache_k, cache_v, state_wkv, state_shift, page_table, ffa_w_gu, ffa_w_down, ffb_w_gu, ffb_w_down, ln1_g, ln1_b, ln2_g, ln2_b, ln3_g, ln3_b, w_in, w_out, da_lq1, da_lk1, da_lq2, da_lk2, da_subln_g, rw_mu, rw_w0, rw_w2, rw_a0, rw_a2, rw_g2, rw_k_k, rw_k_a, rw_r_k, rw_lnx_g, rw_lnx_b):
    bsz, seq, d_model = x_prompt.shape
    db, dec_seq, _ = x_sample.shape
    assert dec_seq == 1, "the sample group is written for one new token per sequence"
    depth = w_in.shape[0]
    n_pool, page, n_heads, _ = cache_k.shape[1:]
    n_pages = page_table.shape[1]
    rd = rw_w0.shape[1]
    rw_cols = rw_mu.shape[1]
    rw_heads = rd // RW_HEAD
    d_ff = ffa_w_down.shape[1]
    alpha = (2 * depth) ** 0.25
    width = n_heads * V7X_LANES

    cos_p, sin_p = _rope_tables(jnp.arange(seq))
    cos_s, sin_s = _rope_tables(jnp.full((db,), n_pages * page, jnp.int32))
    ones_bd = jnp.asarray(np.kron(np.eye(rw_heads, dtype=np.float32),
                                  np.ones((RW_HEAD, RW_HEAD), np.float32))).astype(BF16)

    m_p = bsz * seq
    tm_p = _tile(m_p, 512)
    tf = _tile(d_ff, 256)
    y_p = x_prompt.reshape(m_p, d_model)
    y_s = x_sample.reshape(db, d_model)
    outs = [[] for _ in range(8)]
    row = lambda a: a[None, :]

    for l in range(depth):
        lam_init = 0.8 - 0.6 * math.exp(-0.3 * l)
        wa_gu, wa_dn = ffa_w_gu[l].astype(BF16), ffa_w_down[l].astype(BF16)
        wb_gu, wb_dn = ffb_w_gu[l].astype(BF16), ffb_w_down[l].astype(BF16)
        wi, wo = w_in[l].astype(BF16), w_out[l].astype(BF16)
        lam_vecs = [row(da_lq1[l]), row(da_lk1[l]), row(da_lq2[l]), row(da_lk2[l]), row(da_subln_g[l])]
        rw_weights = _rwkv_weights(dict(rw_mu=rw_mu[l], rw_w0=rw_w0[l], rw_w2=rw_w2[l], rw_a0=rw_a0[l],
                                        rw_a2=rw_a2[l], rw_g2=rw_g2[l], rw_k_k=rw_k_k[l],
                                        rw_k_a=rw_k_a[l], rw_r_k=rw_r_k[l]), rd)
        rw_consts = rw_weights + [ones_bd]
        ln = [row(a[l]) for a in (ln1_g, ln1_b, ln2_g, ln2_b, ln3_g, ln3_b)]
        lnx = [row(rw_lnx_g[l]), row(rw_lnx_b[l])]

        h1 = _ffn_ln(y_p, wa_gu, wa_dn, ln[0], ln[1], alpha=alpha, tm=tm_p, tf=tf)
        shift0 = jnp.zeros((bsz, 1, rw_cols), F32)
        q, k, kb, v, vb, shift_p, r_, k2_, v_, na_, b_, lw_, gate, bonus = _inproj(
            h1, wi, cos_p, sin_p, shift0, rw_consts, n_heads=n_heads, tm=tm_p, seq_len=seq,
            rw_dtype=BF16)
        oda = _attn_prompt(q.reshape(bsz, seq, width), kb.reshape(bsz, seq, width),
                           vb.reshape(bsz, seq, width), *lam_vecs,
                           n_heads=n_heads, tq=512, rs=128, lam_init=lam_init)
        seq3 = lambda a: a.reshape(bsz, seq, rd)
        orw, wkv_p = _rwkv_chunked(seq3(r_), seq3(k2_), seq3(v_), seq3(na_), seq3(b_), seq3(lw_), bb=4)
        h2 = _outproj_ln(oda.reshape(m_p, width), orw.reshape(m_p, rd), bonus, gate, h1, wo, *lnx,
                         ones_bd, ln[2], ln[3], alpha=alpha, tm=tm_p)
        y_p = _ffn_ln(h2, wb_gu, wb_dn, ln[4], ln[5], alpha=alpha, tm=tm_p, tf=tf)
        outs[0].append(k.reshape(bsz, seq, n_heads, 2 * DA_QK))
        outs[1].append(v.reshape(bsz, seq, n_heads, DA_V))
        outs[2].append(wkv_p)
        outs[3].append(shift_p)

        h1 = _ffn_ln(y_s, wa_gu, wa_dn, ln[0], ln[1], alpha=alpha, tm=db, tf=tf)
        q, k, kb, v, vb, shift_s, r_, k2_, v_, na_, b_, lw_, gate, bonus = _inproj(
            h1, wi, cos_s, sin_s, state_shift[l].reshape(db, rw_cols), rw_consts, n_heads=n_heads,
            tm=db, seq_len=None, rw_dtype=F32)
        oda = _attn_decode(q, kb, vb, cache_k.reshape(depth * n_pool, page * n_heads, V7X_LANES),
                           cache_v.reshape(depth * n_pool, page * n_heads, DA_V), page_table, l * n_pool,
                           *lam_vecs, n_heads=n_heads, gp=16, lam_init=lam_init)
        orw, wkv_s = _rwkv_step(state_wkv[l], r_, k2_, v_, na_, b_, lw_)
        h2 = _outproj_ln(oda, orw, bonus, gate, h1, wo, *lnx, ones_bd, ln[2], ln[3],
                         alpha=alpha, tm=db)
        y_s = _ffn_ln(h2, wb_gu, wb_dn, ln[4], ln[5], alpha=alpha, tm=db, tf=tf)
        outs[4].append(k.reshape(db, 1, n_heads, 2 * DA_QK))
        outs[5].append(v.reshape(db, 1, n_heads, DA_V))
        outs[6].append(wkv_s)
        outs[7].append(shift_s.reshape(db, 1, rw_cols))

    stacked = [jnp.stack(o) for o in outs]
    return (y_p.reshape(bsz, seq, d_model), y_s.reshape(db, 1, d_model), *stacked)
```

```python
import functools
import math

import jax
import jax.numpy as jnp
import numpy as np
from jax import lax
from jax.experimental import pallas as pl
from jax.experimental.pallas import tpu as pltpu

F32 = jnp.float32
BF16 = jnp.bfloat16

DA_QK = 64
DA_V = 2 * DA_QK
RW_HEAD = 64
RW_W_LORA = 64
RW_A_LORA = 64
RW_G_LORA = 128
RW_GN_EPS = 64e-5
ROPE_THETA = 10000.0
LN_EPS = 1e-5
NEG = -1e30
LOG2E = math.log2(math.e)

V7X_LANES = 128
V7X_VMEM_LIMIT_BYTES = 56 * 1024 * 1024

RW_CHUNK = 64
RW_GROUP = 4
RW_NG = RW_CHUNK * RW_GROUP


def _cparams(semantics):
    return pltpu.CompilerParams(dimension_semantics=semantics,
                                vmem_limit_bytes=V7X_VMEM_LIMIT_BYTES)


def _dot(a, b):
    return jnp.dot(a.astype(BF16), b.astype(BF16), preferred_element_type=F32)


def _dot_nt(a, b):
    return lax.dot_general(a.astype(BF16), b.astype(BF16), (((1,), (1,)), ((), ())),
                           preferred_element_type=F32)


def _split(x):
    hi = x.astype(BF16)
    lo = (x - hi.astype(F32)).astype(BF16)
    return hi, lo


def _dot_exact_rhs(a, b):
    ah, al = _split(a)
    return (jnp.dot(ah, b, preferred_element_type=F32)
            + jnp.dot(al, b, preferred_element_type=F32))


def _layernorm(y, g, b):
    mu = jnp.mean(y, axis=-1, keepdims=True)
    d = y - mu
    var = jnp.mean(d * d, axis=-1, keepdims=True)
    return d * lax.rsqrt(var + LN_EPS) * g + b


def _ffn_ln_kernel(x_ref, wgu_ref, wd_ref, g_ref, b_ref, o_ref, *, d_ff, tf, alpha):
    x = x_ref[...]
    xb = x.astype(BF16)

    def activation(c):
        gate = jnp.dot(xb, wgu_ref[:, c * tf:(c + 1) * tf], preferred_element_type=F32)
        up = jnp.dot(xb, wgu_ref[:, d_ff + c * tf:d_ff + (c + 1) * tf], preferred_element_type=F32)
        return (gate * jax.nn.sigmoid(gate) * up).astype(BF16)

    nf = d_ff // tf
    act = activation(0)
    acc = None
    for c in range(nf):
        nxt = activation(c + 1) if c + 1 < nf else None
        part = jnp.dot(act, wd_ref[c * tf:(c + 1) * tf, :], preferred_element_type=F32)
        acc = part if acc is None else acc + part
        act = nxt
    o_ref[...] = _layernorm(alpha * x + 0.5 * acc, g_ref[...], b_ref[...])


def _ffn_ln(x, w_gu, w_down, g, b, *, alpha, tm, tf):
    m, d = x.shape
    f = w_down.shape[0]
    assert m % tm == 0 and f % tf == 0
    const = lambda shape: pl.BlockSpec(shape, lambda i: (0, 0))
    return pl.pallas_call(
        functools.partial(_ffn_ln_kernel, d_ff=f, tf=tf, alpha=alpha),
        grid=(m // tm,),
        in_specs=[pl.BlockSpec((tm, d), lambda i: (i, 0)), const(w_gu.shape), const(w_down.shape),
                  const((1, d)), const((1, d))],
        out_specs=pl.BlockSpec((tm, d), lambda i: (i, 0)),
        out_shape=jax.ShapeDtypeStruct((m, d), F32),
        compiler_params=_cparams(("parallel",)),
        name="ffn_ln",
    )(x, w_gu, w_down, g, b)


def _inproj_kernel(h_ref, w_ref, cos_ref, sin_ref, prev_ref, mu_ref, w0_ref, w2p_ref, a0_ref,
                   a2p_ref, g2_ref, kk_ref, ka_ref, rk_ref, ones_ref,
                   q_ref, k_ref, kb_ref, v_ref, vb_ref, shift_ref, *rest,
                   n_heads, scale, steps_per_seq):
    rw_outs, carry_ref, xs_ref = rest[:-2], rest[-2], rest[-1]
    hb = h_ref[...].astype(BF16)
    tm = hb.shape[0]
    cos = cos_ref[...]
    sin = sin_ref[...]
    lane = lax.broadcasted_iota(jnp.int32, cos.shape, 1)
    first_half = (lane % DA_QK) < (DA_QK // 2)
    qc = n_heads * 2 * DA_QK

    def rope_head(x):
        rot = jnp.where(first_half, pltpu.roll(x, V7X_LANES - DA_QK // 2, 1),
                        pltpu.roll(x, DA_QK // 2, 1))
        return x * cos + rot * sin

    def project_q():
        q = jnp.dot(hb, w_ref[:, 0:qc], preferred_element_type=F32)
        for h in range(n_heads):
            sl = slice(h * V7X_LANES, (h + 1) * V7X_LANES)
            q_ref[:, sl] = (rope_head(q[:, sl]) * scale).astype(BF16)

    def project_kv():
        k = jnp.dot(hb, w_ref[:, qc:2 * qc], preferred_element_type=F32)
        v = jnp.dot(hb, w_ref[:, 2 * qc:2 * qc + vc], preferred_element_type=F32)
        for h in range(n_heads):
            kh = rope_head(k[:, h * V7X_LANES:(h + 1) * V7X_LANES])
            kb_ref[:, h * V7X_LANES:(h + 1) * V7X_LANES] = kh.astype(BF16)
            k_ref[pl.ds(h, tm, stride=n_heads), :] = kh
            v_ref[pl.ds(h, tm, stride=n_heads), :] = v[:, h * DA_V:(h + 1) * DA_V]
        vb_ref[...] = v.astype(BF16)

    vc = n_heads * DA_V
    def project_rw():
        return jnp.dot(hb, w_ref[:, 2 * qc + vc:], preferred_element_type=F32)

    if steps_per_seq is None:
        x = project_rw()
        prev = prev_ref[...]
        shift_ref[...] = x
        before_dots, after_dots = project_q, project_kv
    else:
        i = pl.program_id(0)

        @pl.when(i == 0)
        def _():
            xs_ref[...] = jnp.zeros(xs_ref.shape, F32)
            carry_ref[...] = jnp.zeros(carry_ref.shape, F32)

        @pl.when((i - 1) % steps_per_seq == 0)
        def _():
            carry_ref[...] = prev_ref[0]

        x = xs_ref[...]
        row = lax.broadcasted_iota(jnp.int32, x.shape, 0)
        prev = jnp.where(row == 0, carry_ref[...], pltpu.roll(x, 1, 0))
        carry_ref[...] = x[tm - 1:tm, :]
        shift_ref[0] = x[tm - 1:tm, :]
        parked = []

        before_dots = None

        def after_dots():
            parked.append(project_rw())
            project_q()
            project_kv()

    vals = _rwkv_prep_math(x, prev, mu_ref[...], w0_ref[...], w2p_ref[...], a0_ref[...],
                           a2p_ref[...], g2_ref[...], kk_ref[...], ka_ref[...], rk_ref[...],
                           ones_ref[...], before_dots=before_dots, after_dots=after_dots)
    for o_ref, val in zip(rw_outs, vals):
        o_ref[...] = val.astype(o_ref.dtype)
    if steps_per_seq is not None:
        xs_ref[...] = parked[0]


def _inproj(h, w_in, cos_tab, sin_tab, prev, rw_consts, *, n_heads, tm, seq_len, rw_dtype):
    m, d = h.shape
    cols = w_in.shape[1]
    qc = n_heads * 2 * DA_QK
    vc = n_heads * DA_V
    rw = cols - 2 * qc - vc
    rd = rw_consts[-1].shape[0]
    n_pos_blocks = cos_tab.shape[0] // tm
    n = m // tm
    if seq_len is None:
        steps_per_seq = None
        n_steps = n
        tok = lambda i: (i, 0)
        lag = tok
        prev_spec = pl.BlockSpec((tm, rw), tok)
        shift_spec, shift_shape = pl.BlockSpec((tm, rw), tok), (m, rw)
    else:
        assert seq_len % tm == 0
        steps_per_seq = seq_len // tm
        n_steps = n + 1
        tok = lambda i: (jnp.minimum(i, n - 1), 0)
        lag = lambda i: (jnp.maximum(i - 1, 0), 0)
        prev_spec = pl.BlockSpec((1, 1, rw),
                                 lambda i: (jnp.maximum(i - 1, 0) // steps_per_seq, 0, 0))
        shift_spec, shift_shape = prev_spec, (m // seq_len, 1, rw)
    pos = lambda i: (tok(i)[0] % n_pos_blocks, 0)
    rw_dtypes = [F32 if i == 5 else rw_dtype for i in range(8)]
    return pl.pallas_call(
        functools.partial(_inproj_kernel, n_heads=n_heads, scale=DA_QK ** -0.5 * LOG2E,
                          steps_per_seq=steps_per_seq),
        grid=(n_steps,),
        in_specs=[
            pl.BlockSpec((tm, d), tok),
            pl.BlockSpec((d, cols), lambda i: (0, 0)),
            pl.BlockSpec((tm, V7X_LANES), pos),
            pl.BlockSpec((tm, V7X_LANES), pos),
            prev_spec,
        ] + _weight_specs(rw_consts, 1),
        out_specs=[
            pl.BlockSpec((tm, qc), tok), pl.BlockSpec((tm * n_heads, 2 * DA_QK), tok),
            pl.BlockSpec((tm, qc), tok), pl.BlockSpec((tm * n_heads, DA_V), tok),
            pl.BlockSpec((tm, vc), tok), shift_spec,
        ] + [pl.BlockSpec((tm, rd), lag)] * 8,
        out_shape=[
            jax.ShapeDtypeStruct((m, qc), BF16), jax.ShapeDtypeStruct((m * n_heads, 2 * DA_QK), F32),
            jax.ShapeDtypeStruct((m, qc), BF16), jax.ShapeDtypeStruct((m * n_heads, DA_V), F32),
            jax.ShapeDtypeStruct((m, vc), BF16), jax.ShapeDtypeStruct(shift_shape, F32),
        ] + [jax.ShapeDtypeStruct((m, rd), dt) for dt in rw_dtypes],
        scratch_shapes=[pltpu.VMEM((1, rw), F32), pltpu.VMEM((tm, rw), F32)],
        compiler_params=_cparams(("arbitrary",)),
        name="inproj_rope",
    )(h, w_in, cos_tab, sin_tab, prev, *rw_consts)


def _rope_tables(positions):
    half = DA_QK // 2
    inv = ROPE_THETA ** (-jnp.arange(half, dtype=F32) / half)
    ang = positions.astype(F32)[:, None] * inv[None, :]
    cos = jnp.cos(ang)
    sin = jnp.sin(ang)
    cos_tab = jnp.concatenate([cos, cos, cos, cos], axis=1)
    sin_tab = jnp.concatenate([-sin, sin, -sin, sin], axis=1)
    return cos_tab, sin_tab


def _lambda(lq1_ref, lk1_ref, lq2_ref, lk2_ref, lam_init):
    s1 = jnp.sum(lq1_ref[...] * lk1_ref[...], axis=-1, keepdims=True)
    s2 = jnp.sum(lq2_ref[...] * lk2_ref[...], axis=-1, keepdims=True)
    return jnp.exp(s1) - jnp.exp(s2) + lam_init


def _subln(o, g, lam_init):
    return o * lax.rsqrt(jnp.mean(o * o, axis=-1, keepdims=True) + LN_EPS) * g * (1.0 - lam_init)


def _attn_prompt_kernel(lq1_ref, lk1_ref, lq2_ref, lk2_ref, g_ref, q_ref, k_ref, v_ref, o_ref,
                        qm_ref, vx_ref, s_ref, p_ref, m_ref, acc_ref, *, tq, rs, lam_init):
    i = pl.program_id(2)

    @pl.when(i == 0)
    def _():
        vx_ref[:, 0:DA_V] = v_ref[0]
        vx_ref[:, DA_V:] = jnp.ones((vx_ref.shape[0], V7X_LANES), BF16)

    q = q_ref[0].astype(F32)
    lane = lax.broadcasted_iota(jnp.int32, q.shape, 1)
    qm_ref[0] = jnp.where(lane < DA_QK, q, 0.0).astype(BF16)
    qm_ref[1] = jnp.where(lane >= DA_QK, q, 0.0).astype(BF16)
    m_ref[...] = jnp.full(m_ref.shape, NEG, F32)
    acc_ref[...] = jnp.zeros(acc_ref.shape, F32)
    nt = tq // V7X_LANES

    kw = 2 * V7X_LANES

    def score_parts(j, buf):
        def part(mp, t):
            kblk = k_ref[0, pl.ds(pl.multiple_of(j * tq + t * kw, kw), kw), :]
            s_ref[buf, mp, :, t * kw:(t + 1) * kw] = lax.dot_general(
                qm_ref[mp], kblk, (((1,), (1,)), ((), ())), preferred_element_type=F32)
        return [functools.partial(part, mp, t) for t in range(tq // kw) for mp in range(2)]

    def scores(j, buf):
        for part in score_parts(j, buf):
            part()

    def softmax_pv(j, buf, diagonal, fillers=()):
        off = pl.multiple_of(j * tq, tq)
        fillers = list(fillers)
        for r in range(tq // rs):
            if fillers:
                fillers.pop(0)()
            rows = slice(r * rs, (r + 1) * rs)
            n_vis = min(nt, pl.cdiv((r + 1) * rs, V7X_LANES)) if diagonal else nt
            vblk = vx_ref[pl.ds(off, n_vis * V7X_LANES), :]
            def tile(mp, t):
                s = s_ref[buf, mp, rows, t * V7X_LANES:(t + 1) * V7X_LANES]
                if diagonal and (t + 1) * V7X_LANES > r * rs + 1:
                    row = lax.broadcasted_iota(jnp.int32, s.shape, 0) + r * rs
                    col = lax.broadcasted_iota(jnp.int32, s.shape, 1) + t * V7X_LANES
                    s = jnp.where(col <= row, s, NEG)
                return s

            for mp in range(2):
                mx = functools.reduce(jnp.maximum, [tile(mp, t) for t in range(n_vis)])
                m_old = m_ref[mp, rows, :]
                m_new = jnp.maximum(m_old, jnp.max(mx, axis=1, keepdims=True))
                for t in range(n_vis):
                    p_ref[mp, rows, t * V7X_LANES:(t + 1) * V7X_LANES] = jnp.exp2(
                        tile(mp, t) - m_new).astype(BF16)
                c = jnp.exp2(m_old - m_new)
                m_ref[mp, rows, :] = m_new
                pv = jnp.dot(p_ref[mp, rows, 0:n_vis * V7X_LANES], vblk, preferred_element_type=F32)
                acc_ref[mp, rows, :] = jnp.concatenate([c, c], axis=1) * acc_ref[mp, rows, :] + pv

    scores(0, 0)

    def body(jj, carry):
        j = 2 * jj
        softmax_pv(j, 0, False, score_parts(j + 1, 1))
        softmax_pv(j + 1, 1, False, score_parts(j + 2, 0))
        return carry

    lax.fori_loop(0, i // 2, body, 0)

    @pl.when(i % 2 == 0)
    def _():
        softmax_pv(i, 0, True)

    @pl.when(i % 2 == 1)
    def _():
        softmax_pv(i - 1, 0, False, score_parts(i, 1))
        softmax_pv(i, 1, True)

    lam = _lambda(lq1_ref, lk1_ref, lq2_ref, lk2_ref, lam_init)
    o = (acc_ref[0, :, 0:DA_V] / acc_ref[0, :, DA_V:]
         - lam * (acc_ref[1, :, 0:DA_V] / acc_ref[1, :, DA_V:]))
    o_ref[0] = _subln(o, g_ref[...], lam_init).astype(o_ref.dtype)


def _attn_prompt(q, kb, vb, lq1, lk1, lq2, lk2, subln_g, *, n_heads, tq, rs, lam_init):
    b, s, _ = q.shape
    vec = lambda n: pl.BlockSpec((1, n), lambda bi, h, i: (0, 0))
    return pl.pallas_call(
        functools.partial(_attn_prompt_kernel, tq=tq, rs=rs, lam_init=lam_init),
        grid=(b, n_heads, s // tq),
        in_specs=[
            vec(DA_QK), vec(DA_QK), vec(DA_QK), vec(DA_QK), vec(DA_V),
            pl.BlockSpec((1, tq, V7X_LANES), lambda bi, h, i: (bi, i, h)),
            pl.BlockSpec((1, s, V7X_LANES), lambda bi, h, i: (bi, 0, h)),
            pl.BlockSpec((1, s, DA_V), lambda bi, h, i: (bi, 0, h)),
        ],
        out_specs=pl.BlockSpec((1, tq, DA_V), lambda bi, h, i: (bi, i, h)),
        out_shape=jax.ShapeDtypeStruct((b, s, n_heads * DA_V), BF16),
        scratch_shapes=[pltpu.VMEM((2, tq, V7X_LANES), BF16), pltpu.VMEM((s, DA_V + V7X_LANES), BF16),
                        pltpu.VMEM((2, 2, tq, tq), F32), pltpu.VMEM((2, tq, tq), BF16),
                        pltpu.VMEM((2, tq, V7X_LANES), F32),
                        pltpu.VMEM((2, tq, DA_V + V7X_LANES), F32)],
        compiler_params=_cparams(("parallel", "parallel", "arbitrary")),
        name="attn_prompt",
    )(lq1, lk1, lq2, lk2, subln_g, q, kb, vb)


def _attn_decode_kernel(pt_ref, lq1_ref, lk1_ref, lq2_ref, lk2_ref, g_ref, q_ref, kn_ref, vn_ref,
                        *rest, n_heads, n_steps, gp, lam_init):
    del pt_ref
    kp_refs, vp_refs = rest[:gp], rest[gp:2 * gp]
    o_ref, qm_ref, ex_ref, m_ref, l_ref, acc_ref = rest[2 * gp:]
    p = pl.program_id(1)
    nj = 2 * n_heads
    dn = (((1,), (1,)), ((), ()))
    sub = lax.broadcasted_iota(jnp.int32, (8, V7X_LANES), 0)
    lane = lax.broadcasted_iota(jnp.int32, (8, V7X_LANES), 1)
    live = (sub % n_heads) == (lane // 2)

    def head_rows(x):
        out = jnp.zeros((8, V7X_LANES), F32)
        for h in range(n_heads):
            xh = jnp.broadcast_to(x[:, h * V7X_LANES:(h + 1) * V7X_LANES], (8, V7X_LANES))
            out = jnp.where(sub == h, xh, out)
        return out

    def per_row(x, mp):
        pick = lane == 2 * (sub % n_heads) + mp
        return jnp.broadcast_to(jnp.sum(jnp.where(pick, x, 0.0), axis=1, keepdims=True),
                                (8, V7X_LANES))

    def softmax_rows(s, rows_live):
        r = s.shape[0]
        s3 = jnp.where(rows_live, s.reshape(r // 8, 8, V7X_LANES), NEG)
        m_g = jnp.max(jnp.max(s3, axis=0), axis=0, keepdims=True)
        pr = jnp.exp2(s3 - m_g).reshape(r, V7X_LANES)
        return m_g, pr, jnp.sum(pr, axis=0, keepdims=True)

    def weighted_values(pb, v):
        r = v.shape[0]
        return [jnp.sum((pb[:, mp * V7X_LANES:(mp + 1) * V7X_LANES] * v)
                        .reshape(r // 8, 8, V7X_LANES), axis=0) for mp in range(2)]

    @pl.when(p == 0)
    def _():
        e_r = lax.broadcasted_iota(jnp.int32, ex_ref.shape, 0)
        e_c = lax.broadcasted_iota(jnp.int32, ex_ref.shape, 1)
        ex_ref[...] = jnp.where((e_r < nj) & (e_r % 2 == e_c // V7X_LANES), 1.0, 0.0).astype(BF16)
        qrows = jnp.zeros((8, V7X_LANES), F32)
        q = q_ref[0].astype(F32)
        for h in range(n_heads):
            qh = jnp.broadcast_to(q[:, h * V7X_LANES:(h + 1) * V7X_LANES], (8, V7X_LANES))
            qrows = jnp.where(sub // 2 == h, qh, qrows)
        qrows = jnp.where(lane // DA_QK == sub % 2, qrows, 0.0)
        qm_ref[...] = jnp.zeros(qm_ref.shape, BF16)
        qm_ref[0:8, :] = qrows.astype(BF16)
        kn = head_rows(kn_ref[0].astype(F32))
        s_new = lax.dot_general(kn.astype(BF16), qm_ref[...], dn, preferred_element_type=F32)
        m_g, pr, l_g = softmax_rows(s_new, live & (sub < n_heads))
        m_ref[...] = m_g
        l_ref[...] = l_g
        pb = jnp.dot(pr.astype(BF16), ex_ref[...], preferred_element_type=F32)
        parts = weighted_values(pb, head_rows(vn_ref[0].astype(F32)))
        for mp in range(2):
            acc_ref[mp] = parts[mp]

    qm = qm_ref[...]
    s_pg = [lax.dot_general(kp_refs[g][0].astype(BF16), qm, dn, preferred_element_type=F32)
            for g in range(gp)]
    stats = [softmax_rows(s, live) for s in s_pg]
    ex = ex_ref[...]
    pb_pg = [jnp.dot(st[1].astype(BF16), ex, preferred_element_type=F32) for st in stats]
    part_pg = [weighted_values(pb, vp_refs[g][0]) for g, pb in enumerate(pb_pg)]
    m_old = m_ref[...]
    m_new = functools.reduce(jnp.maximum, [st[0] for st in stats], m_old)
    c = jnp.exp2(m_old - m_new)
    w_pg = [jnp.exp2(st[0] - m_new) for st in stats]
    l_ref[...] = c * l_ref[...] + functools.reduce(
        jnp.add, [w * st[2] for w, st in zip(w_pg, stats)])
    m_ref[...] = m_new
    for mp in range(2):
        acc = per_row(c, mp) * acc_ref[mp]
        for g in range(gp):
            acc = acc + per_row(w_pg[g], mp) * part_pg[g][mp]
        acc_ref[mp] = acc

    @pl.when(p == n_steps - 1)
    def _():
        lam = _lambda(lq1_ref, lk1_ref, lq2_ref, lk2_ref, lam_init)
        on = [acc_ref[mp] / per_row(l_ref[...], mp) for mp in range(2)]
        for h in range(n_heads):
            o1, o2 = [functools.reduce(jnp.add, [x[i:i + 1, :] for i in range(h, 8, n_heads)])
                      for x in on]
            o_ref[0, :, h * V7X_LANES:(h + 1) * V7X_LANES] = _subln(
                o1 - lam * o2, g_ref[...], lam_init).astype(o_ref.dtype)


def _attn_decode(q, kb, vb, cache_k, cache_v, page_table, page_base, lq1, lk1, lq2, lk2, subln_g,
                 *, n_heads, gp, lam_init):
    db, width = q.shape
    n_pages = page_table.shape[1]
    rows = cache_k.shape[1]
    assert 8 % n_heads == 0 and rows % 8 == 0
    n_steps = n_pages // gp
    vec = lambda n: pl.BlockSpec((1, n), lambda bi, p, pt: (0, 0))
    tok = pl.BlockSpec((1, 1, width), lambda bi, p, pt: (bi, 0, 0))
    pages = [pl.BlockSpec((1, rows, V7X_LANES),
                          lambda bi, p, pt, g=g: (page_base + pt[bi, p * gp + g], 0, 0))
             for g in range(gp)]
    grid_spec = pltpu.PrefetchScalarGridSpec(
        num_scalar_prefetch=1,
        grid=(db, n_steps),
        in_specs=[vec(DA_QK), vec(DA_QK), vec(DA_QK), vec(DA_QK), vec(DA_V), tok, tok, tok]
                 + pages + pages,
        out_specs=tok,
        scratch_shapes=[pltpu.VMEM((V7X_LANES, V7X_LANES), BF16),
                        pltpu.VMEM((V7X_LANES, 2 * V7X_LANES), BF16),
                        pltpu.VMEM((1, V7X_LANES), F32), pltpu.VMEM((1, V7X_LANES), F32),
                        pltpu.VMEM((2, 8, V7X_LANES), F32)],
    )
    out = pl.pallas_call(
        functools.partial(_attn_decode_kernel, n_heads=n_heads, n_steps=n_steps, gp=gp,
                          lam_init=lam_init),
        grid_spec=grid_spec,
        out_shape=jax.ShapeDtypeStruct((db, 1, width), BF16),
        compiler_params=_cparams(("parallel", "arbitrary")),
        name="attn_decode",
    )(page_table, lq1, lk1, lq2, lk2, subln_g,
      q.reshape(db, 1, width), kb.reshape(db, 1, width), vb.reshape(db, 1, width),
      *([cache_k] * gp), *([cache_v] * gp))
    return out.reshape(db, width)


def _rwkv_prep_math(x, prev, mu, w0, w2p, a0, a2p, g2, k_k, k_a, r_k, ones_bd,
                    before_dots=None, after_dots=None):
    rd = ones_bd.shape[0]
    xm = x + (prev - x) * mu
    r = xm[:, 0:rd]
    k = xm[:, rd:2 * rd]
    v = xm[:, 2 * rd:3 * rd]
    xwa = xm[:, 3 * rd:3 * rd + RW_W_LORA + RW_A_LORA]
    xg = xm[:, 3 * rd + RW_W_LORA + RW_A_LORA:]
    tw = jnp.tanh(xwa)
    sg = jax.nn.sigmoid(xg)
    kk = k * k_k
    kk_sq = kk * kk
    if before_dots is not None:
        before_dots()
    lora_w = _dot(tw, w2p)
    lora_a = _dot(xwa, a2p)
    g = _dot(sg, g2)
    n2 = _dot(kk_sq, ones_bd)
    if after_dots is not None:
        after_dots()
    zw = -(w0 + lora_w)
    softplus = jnp.maximum(zw, 0.0) + jnp.log(1.0 + jnp.exp(-jnp.abs(zw)))
    logw = -jnp.exp(-softplus - 0.5)
    a = jax.nn.sigmoid(a0 + lora_a)
    kk = kk / jnp.maximum(jnp.sqrt(n2), 1e-12)
    k2 = k * (1.0 + (a - 1.0) * k_a)
    bonus = _dot(r * k2 * r_k, ones_bd) * v
    return r, k2, v, -kk, kk * a, logw, g, bonus


def _rwkv_weights(p, rd):
    zeros_w = jnp.zeros((RW_A_LORA, rd), F32)
    zeros_a = jnp.zeros((RW_W_LORA, rd), F32)
    return [p['rw_mu'][None, :], p['rw_w0'][None, :],
            jnp.concatenate([p['rw_w2'], zeros_w], axis=0).astype(BF16), p['rw_a0'][None, :],
            jnp.concatenate([zeros_a, p['rw_a2']], axis=0).astype(BF16), p['rw_g2'].astype(BF16),
            p['rw_k_k'][None, :], p['rw_k_a'][None, :], p['rw_r_k'].reshape(1, rd)]


def _weight_specs(weights, nargs):
    zero = {1: lambda i: (0, 0), 2: lambda i, j: (0, 0)}[nargs]
    return [pl.BlockSpec(w.shape, zero) for w in weights]


def _rwkv_masks():
    n = RW_NG
    row = np.arange(n)[:, None]
    col = np.arange(n)[None, :]
    same = (row // RW_CHUNK) == (col // RW_CHUNK)
    masks = [same & (col < row), same & (col <= row), (row // 2 == col // 2) & (col < row)]
    s = 2
    while s < RW_CHUNK:
        masks.append(((row // s) % 2 == 1) & ((col // s) == (row // s) - 1))
        s *= 2
    return np.stack(masks).astype(np.float32)


def _rwkv_chunk_chains(chains, masks_ref, tri, s_olds):
    c, ng = chains[0][0].shape
    lane = lax.broadcasted_iota(jnp.int32, (c, ng), 1)
    row = lax.broadcasted_iota(jnp.int32, (ng, ng), 0)
    col = lax.broadcasted_iota(jnp.int32, (ng, ng), 1)
    d = functools.partial(jnp.dot, preferred_element_type=F32)
    cat0 = lambda *xs: jnp.concatenate(xs, axis=0)
    cat1 = lambda *xs: jnp.concatenate(xs, axis=1)

    def stack(x):
        return cat0(*[jnp.where(lane // RW_HEAD == h, x, 0.0) for h in range(ng // RW_HEAD)])

    def split3(logw):
        hi = logw.astype(BF16)
        rem = logw - hi.astype(F32)
        mid = rem.astype(BF16)
        return hi, mid, (rem - mid.astype(F32)).astype(BF16)

    parts = [split3(ch[5]) for ch in chains]
    cums = [d(tri, hi) + d(tri, mid) + d(tri, lo) for hi, mid, lo in parts]

    def decayed(ch, cum):
        r, k2, v, na, b, logw = ch
        cum_c = cum[c - 1:c, :]
        g_inv = jnp.exp(-cum)
        g_tail = jnp.exp(cum_c - cum)
        return dict(at=stack(na * jnp.exp(cum - logw)), rt=stack(r * jnp.exp(cum)),
                    btkt=cat0(stack(b * g_inv), stack(k2 * g_inv)),
                    bhkh=cat0(stack(b * g_tail), stack(k2 * g_tail)), vm=stack(v),
                    gc=jnp.exp(cum_c))

    q = [decayed(ch, cum) for ch, cum in zip(chains, cums)]
    grams = [_dot_nt(cat0(x['at'], x['rt']), x['btkt']) for x in q]
    strict = masks_ref[0]
    incl = masks_ref[1]
    ns = [g[:ng, :ng] * strict for g in grams]
    gaks = [g[:ng, ng:] * strict for g in grams]
    grbs = [g[ng:, :ng] * incl for g in grams]
    grks = [g[ng:, ng:] * incl for g in grams]

    eye = jnp.where(row == col, 1.0, 0.0)
    xs = [eye + n * masks_ref[2] for n in ns]
    for lvl in range(3, masks_ref.shape[0]):
        lvl_mask = masks_ref[lvl]
        ys = [_dot(x, n * lvl_mask) for x, n in zip(xs, ns)]
        xs = [x + _dot(y, x) for x, y in zip(xs, ys)]

    gvs = [_dot(gak, x['vm']) for gak, x in zip(gaks, q)]
    aus = [_dot(x, cat1(y['at'], gv)) for x, y, gv in zip(xs, q, gvs)]
    ts = [_dot(grb, au) for grb, au in zip(grbs, aus)]
    gkvs = [_dot(grk, x['vm']) for grk, x in zip(grks, q)]
    rbs = [x['rt'] + t[:, :ng] for x, t in zip(q, ts)]
    obs = [t[:, ng:] + gkv for t, gkv in zip(ts, gkvs)]
    ms = [jnp.where(row == col, x['gc'], 0.0) + _dot(au[:, :ng].T, x['bhkh'][:ng])
          for x, au in zip(q, aus)]
    zs = [_dot(cat0(au[:, ng:], x['vm']).T, x['bhkh']) for x, au in zip(q, aus)]

    os_ = [_dot_nt(rb, s_old) + ob for rb, s_old, ob in zip(rbs, s_olds, obs)]
    s_news = [_dot(s_old, m) + z for s_old, m, z in zip(s_olds, ms, zs)]
    o_toks = [functools.reduce(jnp.add, [o[h * c:(h + 1) * c] for h in range(ng // RW_HEAD)])
              for o in os_]
    return o_toks, s_news


def _rwkv_chunk_kernel(r_ref, k_ref, v_ref, na_ref, b_ref, lw_ref, masks_ref, tri_ref,
                       o_ref, sout_ref, s_ref, *, bb, n_groups, n_chunks):
    ci = pl.program_id(1)

    @pl.when(ci == 0)
    def _():
        s_ref[...] = jnp.zeros_like(s_ref)

    ids = [(bi, gi) for bi in range(bb) for gi in range(n_groups)]
    sl = lambda gi: slice(gi * RW_NG, (gi + 1) * RW_NG)
    chains = [tuple(ref[bi, :, sl(gi)].astype(F32)
                    for ref in (r_ref, k_ref, v_ref, na_ref, b_ref, lw_ref))
              for bi, gi in ids]
    o_toks, s_news = _rwkv_chunk_chains(chains, masks_ref, tri_ref[...],
                                        [s_ref[bi * n_groups + gi] for bi, gi in ids])
    for (bi, gi), o_tok, s_new in zip(ids, o_toks, s_news):
        o_ref[bi, :, sl(gi)] = o_tok
        s_ref[bi * n_groups + gi] = s_new

    @pl.when(ci == n_chunks - 1)
    def _():
        for bi, gi in ids:
            s = s_ref[bi * n_groups + gi]
            for h in range(RW_GROUP):
                hs = slice(h * RW_HEAD, (h + 1) * RW_HEAD)
                sout_ref[bi, gi * RW_GROUP + h] = s[hs, hs]


def _rwkv_chunked(r, k2, v, na, b, logw, *, bb):
    bsz, s, rd = r.shape
    n_groups = rd // RW_NG
    n_chunks = s // RW_CHUNK
    n_heads = rd // RW_HEAD
    masks = jnp.asarray(_rwkv_masks())
    tri = jnp.asarray(np.tril(np.ones((RW_CHUNK, RW_CHUNK), np.float32))).astype(BF16)
    tok = pl.BlockSpec((bb, RW_CHUNK, rd), lambda bi, ci: (bi, ci, 0))
    return pl.pallas_call(
        functools.partial(_rwkv_chunk_kernel, bb=bb, n_groups=n_groups, n_chunks=n_chunks),
        grid=(bsz // bb, n_chunks),
        in_specs=[tok] * 6 + [pl.BlockSpec(masks.shape, lambda bi, ci: (0, 0, 0)),
                              pl.BlockSpec(tri.shape, lambda bi, ci: (0, 0))],
        out_specs=[tok, pl.BlockSpec((bb, n_heads, RW_HEAD, RW_HEAD), lambda bi, ci: (bi, 0, 0, 0))],
        out_shape=[jax.ShapeDtypeStruct((bsz, s, rd), F32),
                   jax.ShapeDtypeStruct((bsz, n_heads, RW_HEAD, RW_HEAD), F32)],
        scratch_shapes=[pltpu.VMEM((bb * n_groups, RW_NG, RW_NG), F32)],
        compiler_params=_cparams(("parallel", "arbitrary")),
        name="rwkv_chunked",
    )(r, k2, v, na, b, logw, masks, tri)


def _rwkv_step_kernel(s_ref, r_ref, k_ref, na_ref, b_ref, lw_ref, v_ref, sout_ref, o_ref):
    s = s_ref[...]
    sa = jnp.sum(s * na_ref[...], axis=-1, keepdims=True)
    s_new = s * jnp.exp(lw_ref[...]) + sa * b_ref[...] + v_ref[...] * k_ref[...]
    sout_ref[...] = s_new
    o_ref[...] = jnp.sum(s_new * r_ref[...], axis=-1, keepdims=True)


def _rwkv_step(state, r, k2, v, na, b, logw, *, bs):
    db, nh, hd, _ = state.shape
    assert db % bs == 0
    rowv = lambda x: x.reshape(db, nh, 1, hd)
    rspec = pl.BlockSpec((bs, nh, 1, hd), lambda i: (i, 0, 0, 0))
    cspec = pl.BlockSpec((bs, nh, hd, 1), lambda i: (i, 0, 0, 0))
    sspec = pl.BlockSpec((bs, nh, hd, hd), lambda i: (i, 0, 0, 0))
    s_new, o = pl.pallas_call(
        _rwkv_step_kernel,
        grid=(db // bs,),
        in_specs=[sspec, rspec, rspec, rspec, rspec, rspec, cspec],
        out_specs=[sspec, cspec],
        out_shape=[jax.ShapeDtypeStruct(state.shape, F32), jax.ShapeDtypeStruct((db, nh, hd, 1), F32)],
        compiler_params=_cparams(("parallel",)),
        name="rwkv_step",
    )(state, rowv(r), rowv(k2), rowv(na), rowv(b), rowv(logw), v.reshape(db, nh, hd, 1))
    return o.reshape(db, nh * hd), s_new


def _outproj_ln_kernel(oda_ref, orw_ref, bonus_ref, gate_ref, h_ref, w_ref, lg_ref, lb_ref,
                       ones_ref, g_ref, b_ref, o_ref, *, alpha):
    o = orw_ref[...]
    ones_bd = ones_ref[...]
    inv_n = 1.0 / RW_HEAD
    mu = _dot_exact_rhs(o, ones_bd) * inv_n
    d = o - mu
    var = _dot(d * d, ones_bd) * inv_n
    o = d * lax.rsqrt(var + RW_GN_EPS) * lg_ref[...] + lb_ref[...]
    o_rw = (o + bonus_ref[...]) * gate_ref[...]
    da = oda_ref.shape[1]
    mix = (jnp.dot(oda_ref[...], w_ref[0:da, :], preferred_element_type=F32)
           + jnp.dot(o_rw.astype(BF16), w_ref[da:, :], preferred_element_type=F32))
    o_ref[...] = _layernorm(alpha * h_ref[...] + mix, g_ref[...], b_ref[...])


def _outproj_ln(oda, orw, bonus, gate, h, w_out, lnx_g, lnx_b, ones_bd, g, b, *, alpha, tm):
    m, d = h.shape
    da = oda.shape[1]
    rd = orw.shape[1]
    tok = lambda n: pl.BlockSpec((tm, n), lambda i: (i, 0))
    consts = [w_out, lnx_g, lnx_b, ones_bd, g, b]
    return pl.pallas_call(
        functools.partial(_outproj_ln_kernel, alpha=alpha),
        grid=(m // tm,),
        in_specs=[tok(da), tok(rd), tok(rd), tok(rd), tok(d)] + _weight_specs(consts, 1),
        out_specs=tok(d),
        out_shape=jax.ShapeDtypeStruct((m, d), F32),
        compiler_params=_cparams(("parallel",)),
        name="outproj_ln",
    )(oda, orw, bonus, gate, h, *consts)


def _tile(m, pref):
    return pref if m % pref == 0 else m


def kernel(x_prompt, x_sample, cache_k, cache_v, state_wkv, state_shift, page_table, ffa_w_gu, ffa_w_down, ffb_w_gu, ffb_w_down, ln1_g, ln1_b, ln2_g, ln2_b, ln3_g, ln3_b, w_in, w_out, da_lq1, da_lk1, da_lq2, da_lk2, da_subln_g, rw_mu, rw_w0, rw_w2, rw_a0, rw_a2, rw_g2, rw_k_k, rw_k_a, rw_r_k, rw_lnx_g, rw_lnx_b):
    bsz, seq, d_model = x_prompt.shape
    db, dec_seq, _ = x_sample.shape
    assert dec_seq == 1, "the sample group is written for one new token per sequence"
    depth = w_in.shape[0]
    n_pool, page, n_heads, _ = cache_k.shape[1:]
    n_pages = page_table.shape[1]
    rd = rw_w0.shape[1]
    rw_cols = rw_mu.shape[1]
    rw_heads = rd // RW_HEAD
    d_ff = ffa_w_down.shape[1]
    alpha = (2 * depth) ** 0.25
    width = n_heads * V7X_LANES

    cos_p, sin_p = _rope_tables(jnp.arange(seq))
    cos_s, sin_s = _rope_tables(jnp.full((db,), n_pages * page, jnp.int32))
    ones_bd = jnp.asarray(np.kron(np.eye(rw_heads, dtype=np.float32),
                                  np.ones((RW_HEAD, RW_HEAD), np.float32))).astype(BF16)

    m_p = bsz * seq
    tm_p = _tile(m_p, 512)
    tf = _tile(d_ff, 256)
    y_p = x_prompt.reshape(m_p, d_model)
    y_s = x_sample.reshape(db, d_model)
    outs = [[] for _ in range(8)]
    row = lambda a: a[None, :]

    for l in range(depth):
        lam_init = 0.8 - 0.6 * math.exp(-0.3 * l)
        wa_gu, wa_dn = ffa_w_gu[l].astype(BF16), ffa_w_down[l].astype(BF16)
        wb_gu, wb_dn = ffb_w_gu[l].astype(BF16), ffb_w_down[l].astype(BF16)
        wi, wo = w_in[l].astype(BF16), w_out[l].astype(BF16)
        lam_vecs = [row(da_lq1[l]), row(da_lk1[l]), row(da_lq2[l]), row(da_lk2[l]), row(da_subln_g[l])]
        rw_weights = _rwkv_weights(dict(rw_mu=rw_mu[l], rw_w0=rw_w0[l], rw_w2=rw_w2[l], rw_a0=rw_a0[l],
                                        rw_a2=rw_a2[l], rw_g2=rw_g2[l], rw_k_k=rw_k_k[l],
                                        rw_k_a=rw_k_a[l], rw_r_k=rw_r_k[l]), rd)
        rw_consts = rw_weights + [ones_bd]
        ln = [row(a[l]) for a in (ln1_g, ln1_b, ln2_g, ln2_b, ln3_g, ln3_b)]
        lnx = [row(rw_lnx_g[l]), row(rw_lnx_b[l])]

        h1 = _ffn_ln(y_p, wa_gu, wa_dn, ln[0], ln[1], alpha=alpha, tm=tm_p, tf=tf)
        shift0 = jnp.zeros((bsz, 1, rw_cols), F32)
        q, k, kb, v, vb, shift_p, r_, k2_, v_, na_, b_, lw_, gate, bonus = _inproj(
            h1, wi, cos_p, sin_p, shift0, rw_consts, n_heads=n_heads, tm=tm_p, seq_len=seq,
            rw_dtype=BF16)
        oda = _attn_prompt(q.reshape(bsz, seq, width), kb.reshape(bsz, seq, width),
                           vb.reshape(bsz, seq, width), *lam_vecs,
                           n_heads=n_heads, tq=512, rs=128, lam_init=lam_init)
        seq3 = lambda a: a.reshape(bsz, seq, rd)
        orw, wkv_p = _rwkv_chunked(seq3(r_), seq3(k2_), seq3(v_), seq3(na_), seq3(b_), seq3(lw_), bb=4)
        h2 = _outproj_ln(oda.reshape(m_p, width), orw.reshape(m_p, rd), bonus, gate, h1, wo, *lnx,
                         ones_bd, ln[2], ln[3], alpha=alpha, tm=tm_p)
        y_p = _ffn_ln(h2, wb_gu, wb_dn, ln[4], ln[5], alpha=alpha, tm=tm_p, tf=tf)
        outs[0].append(k.reshape(bsz, seq, n_heads, 2 * DA_QK))
        outs[1].append(v.reshape(bsz, seq, n_heads, DA_V))
        outs[2].append(wkv_p)
        outs[3].append(shift_p)

        h1 = _ffn_ln(y_s, wa_gu, wa_dn, ln[0], ln[1], alpha=alpha, tm=db, tf=tf)
        q, k, kb, v, vb, shift_s, r_, k2_, v_, na_, b_, lw_, gate, bonus = _inproj(
            h1, wi, cos_s, sin_s, state_shift[l].reshape(db, rw_cols), rw_consts, n_heads=n_heads,
            tm=db, seq_len=None, rw_dtype=F32)
        oda = _attn_decode(q, kb, vb, cache_k.reshape(depth * n_pool, page * n_heads, V7X_LANES),
                           cache_v.reshape(depth * n_pool, page * n_heads, DA_V), page_table, l * n_pool,
                           *lam_vecs, n_heads=n_heads, gp=32, lam_init=lam_init)
        orw, wkv_s = _rwkv_step(state_wkv[l], r_, k2_, v_, na_, b_, lw_, bs=_tile(db, 8))
        h2 = _outproj_ln(oda, orw, bonus, gate, h1, wo, *lnx, ones_bd, ln[2], ln[3],
                         alpha=alpha, tm=db)
        y_s = _ffn_ln(h2, wb_gu, wb_dn, ln[4], ln[5], alpha=alpha, tm=db, tf=tf)
        outs[4].append(k.reshape(db, 1, n_heads, 2 * DA_QK))
        outs[5].append(v.reshape(db, 1, n_heads, DA_V))
        outs[6].append(wkv_s)
        outs[7].append(shift_s.reshape(db, 1, rw_cols))

    stacked = [jnp.stack(o) for o in outs]
    return (y_p.reshape(bsz, seq, d_model), y_s.reshape(db, 1, d_model), *stacked)
```

```python
import functools
import math

import jax
import jax.numpy as jnp
import numpy as np
from jax import lax
from jax.experimental import pallas as pl
from jax.experimental.pallas import tpu as pltpu

F32 = jnp.float32
BF16 = jnp.bfloat16

DA_QK = 64
DA_V = 2 * DA_QK
RW_HEAD = 64
RW_W_LORA = 64
RW_A_LORA = 64
RW_G_LORA = 128
RW_GN_EPS = 64e-5
ROPE_THETA = 10000.0
LN_EPS = 1e-5
NEG = -1e30
LOG2E = math.log2(math.e)

V7X_LANES = 128
V7X_VMEM_LIMIT_BYTES = 56 * 1024 * 1024

RW_CHUNK = 64
RW_GROUP = 4
RW_NG = RW_CHUNK * RW_GROUP

TOKEN_TILE = 512
FFN_CHUNK = 256
ATTN_BLOCK = 512
ATTN_ROW_CHUNK = 128
DECODE_PAGES_PER_STEP = 32
RW_SEQS_PER_STEP = 4
RW_STEP_SEQS = 8


def _cparams(semantics):
    return pltpu.CompilerParams(dimension_semantics=semantics,
                                vmem_limit_bytes=V7X_VMEM_LIMIT_BYTES)


def _dot(a, b):
    return jnp.dot(a.astype(BF16), b.astype(BF16), preferred_element_type=F32)


def _dot_nt(a, b):
    return lax.dot_general(a.astype(BF16), b.astype(BF16), (((1,), (1,)), ((), ())),
                           preferred_element_type=F32)


def _split(x):
    hi = x.astype(BF16)
    lo = (x - hi.astype(F32)).astype(BF16)
    return hi, lo


def _dot_exact_rhs(a, b):
    ah, al = _split(a)
    return (jnp.dot(ah, b, preferred_element_type=F32)
            + jnp.dot(al, b, preferred_element_type=F32))


def _layernorm(y, g, b):
    mu = jnp.mean(y, axis=-1, keepdims=True)
    d = y - mu
    var = jnp.mean(d * d, axis=-1, keepdims=True)
    return d * lax.rsqrt(var + LN_EPS) * g + b


def _ffn_ln_kernel(x_ref, wgu_ref, wd_ref, g_ref, b_ref, o_ref, *, d_ff, tf, alpha):
    x = x_ref[...]
    xb = x.astype(BF16)

    def activation(c):
        gate = jnp.dot(xb, wgu_ref[:, c * tf:(c + 1) * tf], preferred_element_type=F32)
        up = jnp.dot(xb, wgu_ref[:, d_ff + c * tf:d_ff + (c + 1) * tf], preferred_element_type=F32)
        return (gate * jax.nn.sigmoid(gate) * up).astype(BF16)

    nf = d_ff // tf
    act = activation(0)
    acc = None
    for c in range(nf):
        nxt = activation(c + 1) if c + 1 < nf else None
        part = jnp.dot(act, wd_ref[c * tf:(c + 1) * tf, :], preferred_element_type=F32)
        acc = part if acc is None else acc + part
        act = nxt
    o_ref[...] = _layernorm(alpha * x + 0.5 * acc, g_ref[...], b_ref[...])


def _ffn_ln(x, w_gu, w_down, g, b, *, alpha, tm, tf):
    m, d = x.shape
    f = w_down.shape[0]
    assert m % tm == 0 and f % tf == 0
    const = lambda shape: pl.BlockSpec(shape, lambda i: (0, 0))
    return pl.pallas_call(
        functools.partial(_ffn_ln_kernel, d_ff=f, tf=tf, alpha=alpha),
        grid=(m // tm,),
        in_specs=[pl.BlockSpec((tm, d), lambda i: (i, 0)), const(w_gu.shape), const(w_down.shape),
                  const((1, d)), const((1, d))],
        out_specs=pl.BlockSpec((tm, d), lambda i: (i, 0)),
        out_shape=jax.ShapeDtypeStruct((m, d), F32),
        compiler_params=_cparams(("parallel",)),
        name="ffn_ln",
    )(x, w_gu, w_down, g, b)


def _inproj_kernel(h_ref, w_ref, cos_ref, sin_ref, prev_ref, mu_ref, w0_ref, w2p_ref, a0_ref,
                   a2p_ref, g2_ref, kk_ref, ka_ref, rk_ref, ones_ref,
                   q_ref, k_ref, kb_ref, v_ref, vb_ref, shift_ref, *rest,
                   n_heads, scale, steps_per_seq):
    rw_outs, carry_ref, xs_ref = rest[:-2], rest[-2], rest[-1]
    hb = h_ref[...].astype(BF16)
    tm = hb.shape[0]
    cos = cos_ref[...]
    sin = sin_ref[...]
    lane = lax.broadcasted_iota(jnp.int32, cos.shape, 1)
    first_half = (lane % DA_QK) < (DA_QK // 2)
    qc = n_heads * 2 * DA_QK

    def rope_head(x):
        rot = jnp.where(first_half, pltpu.roll(x, V7X_LANES - DA_QK // 2, 1),
                        pltpu.roll(x, DA_QK // 2, 1))
        return x * cos + rot * sin

    def project_q():
        q = jnp.dot(hb, w_ref[:, 0:qc], preferred_element_type=F32)
        for h in range(n_heads):
            sl = slice(h * V7X_LANES, (h + 1) * V7X_LANES)
            q_ref[:, sl] = (rope_head(q[:, sl]) * scale).astype(BF16)

    def project_kv():
        k = jnp.dot(hb, w_ref[:, qc:2 * qc], preferred_element_type=F32)
        v = jnp.dot(hb, w_ref[:, 2 * qc:2 * qc + vc], preferred_element_type=F32)
        for h in range(n_heads):
            kh = rope_head(k[:, h * V7X_LANES:(h + 1) * V7X_LANES])
            kb_ref[:, h * V7X_LANES:(h + 1) * V7X_LANES] = kh.astype(BF16)
            k_ref[pl.ds(h, tm, stride=n_heads), :] = kh
            v_ref[pl.ds(h, tm, stride=n_heads), :] = v[:, h * DA_V:(h + 1) * DA_V]
        vb_ref[...] = v.astype(BF16)

    vc = n_heads * DA_V
    def project_rw():
        return jnp.dot(hb, w_ref[:, 2 * qc + vc:], preferred_element_type=F32)

    if steps_per_seq is None:
        x = project_rw()
        prev = prev_ref[...]
        shift_ref[...] = x
        before_dots, after_dots = project_q, project_kv
    else:
        i = pl.program_id(0)

        @pl.when(i == 0)
        def _():
            xs_ref[...] = jnp.zeros(xs_ref.shape, F32)
            carry_ref[...] = jnp.zeros(carry_ref.shape, F32)

        @pl.when((i - 1) % steps_per_seq == 0)
        def _():
            carry_ref[...] = prev_ref[0]

        x = xs_ref[...]
        row = lax.broadcasted_iota(jnp.int32, x.shape, 0)
        prev = jnp.where(row == 0, carry_ref[...], pltpu.roll(x, 1, 0))
        carry_ref[...] = x[tm - 1:tm, :]
        shift_ref[0] = x[tm - 1:tm, :]
        parked = []

        before_dots = None

        def after_dots():
            parked.append(project_rw())
            project_q()
            project_kv()

    vals = _rwkv_prep_math(x, prev, mu_ref[...], w0_ref[...], w2p_ref[...], a0_ref[...],
                           a2p_ref[...], g2_ref[...], kk_ref[...], ka_ref[...], rk_ref[...],
                           ones_ref[...], before_dots=before_dots, after_dots=after_dots)
    for o_ref, val in zip(rw_outs, vals):
        o_ref[...] = val.astype(o_ref.dtype)
    if steps_per_seq is not None:
        xs_ref[...] = parked[0]


def _inproj(h, w_in, cos_tab, sin_tab, prev, rw_consts, *, n_heads, tm, seq_len, rw_dtype):
    m, d = h.shape
    cols = w_in.shape[1]
    qc = n_heads * 2 * DA_QK
    vc = n_heads * DA_V
    rw = cols - 2 * qc - vc
    rd = rw_consts[-1].shape[0]
    n_pos_blocks = cos_tab.shape[0] // tm
    n = m // tm
    if seq_len is None:
        steps_per_seq = None
        n_steps = n
        tok = lambda i: (i, 0)
        lag = tok
        prev_spec = pl.BlockSpec((tm, rw), tok)
        shift_spec, shift_shape = pl.BlockSpec((tm, rw), tok), (m, rw)
    else:
        assert seq_len % tm == 0
        steps_per_seq = seq_len // tm
        n_steps = n + 1
        tok = lambda i: (jnp.minimum(i, n - 1), 0)
        lag = lambda i: (jnp.maximum(i - 1, 0), 0)
        prev_spec = pl.BlockSpec((1, 1, rw),
                                 lambda i: (jnp.maximum(i - 1, 0) // steps_per_seq, 0, 0))
        shift_spec, shift_shape = prev_spec, (m // seq_len, 1, rw)
    pos = lambda i: (tok(i)[0] % n_pos_blocks, 0)
    rw_dtypes = [F32 if i == 5 else rw_dtype for i in range(8)]
    return pl.pallas_call(
        functools.partial(_inproj_kernel, n_heads=n_heads, scale=DA_QK ** -0.5 * LOG2E,
                          steps_per_seq=steps_per_seq),
        grid=(n_steps,),
        in_specs=[
            pl.BlockSpec((tm, d), tok),
            pl.BlockSpec((d, cols), lambda i: (0, 0)),
            pl.BlockSpec((tm, V7X_LANES), pos),
            pl.BlockSpec((tm, V7X_LANES), pos),
            prev_spec,
        ] + _weight_specs(rw_consts, 1),
        out_specs=[
            pl.BlockSpec((tm, qc), tok), pl.BlockSpec((tm * n_heads, 2 * DA_QK), tok),
            pl.BlockSpec((tm, qc), tok), pl.BlockSpec((tm * n_heads, DA_V), tok),
            pl.BlockSpec((tm, vc), tok), shift_spec,
        ] + [pl.BlockSpec((tm, rd), lag)] * 8,
        out_shape=[
            jax.ShapeDtypeStruct((m, qc), BF16), jax.ShapeDtypeStruct((m * n_heads, 2 * DA_QK), F32),
            jax.ShapeDtypeStruct((m, qc), BF16), jax.ShapeDtypeStruct((m * n_heads, DA_V), F32),
            jax.ShapeDtypeStruct((m, vc), BF16), jax.ShapeDtypeStruct(shift_shape, F32),
        ] + [jax.ShapeDtypeStruct((m, rd), dt) for dt in rw_dtypes],
        scratch_shapes=[pltpu.VMEM((1, rw), F32), pltpu.VMEM((tm, rw), F32)],
        compiler_params=_cparams(("arbitrary",)),
        name="inproj_rope",
    )(h, w_in, cos_tab, sin_tab, prev, *rw_consts)


def _rope_tables(positions):
    half = DA_QK // 2
    inv = ROPE_THETA ** (-jnp.arange(half, dtype=F32) / half)
    ang = positions.astype(F32)[:, None] * inv[None, :]
    cos = jnp.cos(ang)
    sin = jnp.sin(ang)
    cos_tab = jnp.concatenate([cos, cos, cos, cos], axis=1)
    sin_tab = jnp.concatenate([-sin, sin, -sin, sin], axis=1)
    return cos_tab, sin_tab


def _lambda(lq1_ref, lk1_ref, lq2_ref, lk2_ref, lam_init):
    s1 = jnp.sum(lq1_ref[...] * lk1_ref[...], axis=-1, keepdims=True)
    s2 = jnp.sum(lq2_ref[...] * lk2_ref[...], axis=-1, keepdims=True)
    return jnp.exp(s1) - jnp.exp(s2) + lam_init


def _subln(o, g, lam_init):
    return o * lax.rsqrt(jnp.mean(o * o, axis=-1, keepdims=True) + LN_EPS) * g * (1.0 - lam_init)


def _attn_prompt_kernel(lq1_ref, lk1_ref, lq2_ref, lk2_ref, g_ref, q_ref, k_ref, v_ref, o_ref,
                        qm_ref, vx_ref, s_ref, p_ref, m_ref, acc_ref, *, tq, rs, lam_init):
    i = pl.program_id(2)

    @pl.when(i == 0)
    def _():
        vx_ref[:, 0:DA_V] = v_ref[0]
        vx_ref[:, DA_V:] = jnp.ones((vx_ref.shape[0], V7X_LANES), BF16)

    q = q_ref[0].astype(F32)
    lane = lax.broadcasted_iota(jnp.int32, q.shape, 1)
    qm_ref[0] = jnp.where(lane < DA_QK, q, 0.0).astype(BF16)
    qm_ref[1] = jnp.where(lane >= DA_QK, q, 0.0).astype(BF16)
    m_ref[...] = jnp.full(m_ref.shape, NEG, F32)
    acc_ref[...] = jnp.zeros(acc_ref.shape, F32)
    nt = tq // V7X_LANES

    kw = 2 * V7X_LANES

    def score_parts(j, buf):
        def part(mp, t):
            kblk = k_ref[0, pl.ds(pl.multiple_of(j * tq + t * kw, kw), kw), :]
            s_ref[buf, mp, :, t * kw:(t + 1) * kw] = lax.dot_general(
                qm_ref[mp], kblk, (((1,), (1,)), ((), ())), preferred_element_type=F32)
        return [functools.partial(part, mp, t) for t in range(tq // kw) for mp in range(2)]

    def scores(j, buf):
        for part in score_parts(j, buf):
            part()

    def softmax_pv(j, buf, diagonal, fillers=()):
        off = pl.multiple_of(j * tq, tq)
        fillers = list(fillers)
        for r in range(tq // rs):
            if fillers:
                fillers.pop(0)()
            rows = slice(r * rs, (r + 1) * rs)
            n_vis = min(nt, pl.cdiv((r + 1) * rs, V7X_LANES)) if diagonal else nt
            vblk = vx_ref[pl.ds(off, n_vis * V7X_LANES), :]
            def tile(mp, t):
                s = s_ref[buf, mp, rows, t * V7X_LANES:(t + 1) * V7X_LANES]
                if diagonal and (t + 1) * V7X_LANES > r * rs + 1:
                    row = lax.broadcasted_iota(jnp.int32, s.shape, 0) + r * rs
                    col = lax.broadcasted_iota(jnp.int32, s.shape, 1) + t * V7X_LANES
                    s = jnp.where(col <= row, s, NEG)
                return s

            for mp in range(2):
                mx = functools.reduce(jnp.maximum, [tile(mp, t) for t in range(n_vis)])
                m_old = m_ref[mp, rows, :]
                m_new = jnp.maximum(m_old, jnp.max(mx, axis=1, keepdims=True))
                for t in range(n_vis):
                    p_ref[mp, rows, t * V7X_LANES:(t + 1) * V7X_LANES] = jnp.exp2(
                        tile(mp, t) - m_new).astype(BF16)
                c = jnp.exp2(m_old - m_new)
                m_ref[mp, rows, :] = m_new
                pv = jnp.dot(p_ref[mp, rows, 0:n_vis * V7X_LANES], vblk, preferred_element_type=F32)
                acc_ref[mp, rows, :] = jnp.concatenate([c, c], axis=1) * acc_ref[mp, rows, :] + pv

    scores(0, 0)

    def body(jj, carry):
        j = 2 * jj
        softmax_pv(j, 0, False, score_parts(j + 1, 1))
        softmax_pv(j + 1, 1, False, score_parts(j + 2, 0))
        return carry

    lax.fori_loop(0, i // 2, body, 0)

    @pl.when(i % 2 == 0)
    def _():
        softmax_pv(i, 0, True)

    @pl.when(i % 2 == 1)
    def _():
        softmax_pv(i - 1, 0, False, score_parts(i, 1))
        softmax_pv(i, 1, True)

    lam = _lambda(lq1_ref, lk1_ref, lq2_ref, lk2_ref, lam_init)
    o = (acc_ref[0, :, 0:DA_V] / acc_ref[0, :, DA_V:]
         - lam * (acc_ref[1, :, 0:DA_V] / acc_ref[1, :, DA_V:]))
    o_ref[0] = _subln(o, g_ref[...], lam_init).astype(o_ref.dtype)


def _attn_prompt(q, kb, vb, lq1, lk1, lq2, lk2, subln_g, *, n_heads, tq, rs, lam_init):
    b, s, _ = q.shape
    vec = lambda n: pl.BlockSpec((1, n), lambda bi, h, i: (0, 0))
    return pl.pallas_call(
        functools.partial(_attn_prompt_kernel, tq=tq, rs=rs, lam_init=lam_init),
        grid=(b, n_heads, s // tq),
        in_specs=[
            vec(DA_QK), vec(DA_QK), vec(DA_QK), vec(DA_QK), vec(DA_V),
            pl.BlockSpec((1, tq, V7X_LANES), lambda bi, h, i: (bi, i, h)),
            pl.BlockSpec((1, s, V7X_LANES), lambda bi, h, i: (bi, 0, h)),
            pl.BlockSpec((1, s, DA_V), lambda bi, h, i: (bi, 0, h)),
        ],
        out_specs=pl.BlockSpec((1, tq, DA_V), lambda bi, h, i: (bi, i, h)),
        out_shape=jax.ShapeDtypeStruct((b, s, n_heads * DA_V), BF16),
        scratch_shapes=[pltpu.VMEM((2, tq, V7X_LANES), BF16), pltpu.VMEM((s, DA_V + V7X_LANES), BF16),
                        pltpu.VMEM((2, 2, tq, tq), F32), pltpu.VMEM((2, tq, tq), BF16),
                        pltpu.VMEM((2, tq, V7X_LANES), F32),
                        pltpu.VMEM((2, tq, DA_V + V7X_LANES), F32)],
        compiler_params=_cparams(("parallel", "parallel", "arbitrary")),
        name="attn_prompt",
    )(lq1, lk1, lq2, lk2, subln_g, q, kb, vb)


def _attn_decode_kernel(pt_ref, lq1_ref, lk1_ref, lq2_ref, lk2_ref, g_ref, q_ref, kn_ref, vn_ref,
                        *rest, n_heads, n_steps, gp, lam_init):
    del pt_ref
    kp_refs, vp_refs = rest[:gp], rest[gp:2 * gp]
    o_ref, qm_ref, ex_ref, m_ref, l_ref, acc_ref = rest[2 * gp:]
    p = pl.program_id(1)
    nj = 2 * n_heads
    dn = (((1,), (1,)), ((), ()))
    sub = lax.broadcasted_iota(jnp.int32, (8, V7X_LANES), 0)
    lane = lax.broadcasted_iota(jnp.int32, (8, V7X_LANES), 1)
    live = (sub % n_heads) == (lane // 2)

    def head_rows(x):
        out = jnp.zeros((8, V7X_LANES), F32)
        for h in range(n_heads):
            xh = jnp.broadcast_to(x[:, h * V7X_LANES:(h + 1) * V7X_LANES], (8, V7X_LANES))
            out = jnp.where(sub == h, xh, out)
        return out

    def per_row(x, mp):
        pick = lane == 2 * (sub % n_heads) + mp
        return jnp.broadcast_to(jnp.sum(jnp.where(pick, x, 0.0), axis=1, keepdims=True),
                                (8, V7X_LANES))

    def softmax_rows(s, rows_live):
        r = s.shape[0]
        s3 = jnp.where(rows_live, s.reshape(r // 8, 8, V7X_LANES), NEG)
        m_g = jnp.max(jnp.max(s3, axis=0), axis=0, keepdims=True)
        pr = jnp.exp2(s3 - m_g).reshape(r, V7X_LANES)
        return m_g, pr, jnp.sum(pr, axis=0, keepdims=True)

    def weighted_values(pb, v):
        r = v.shape[0]
        return [jnp.sum((pb[:, mp * V7X_LANES:(mp + 1) * V7X_LANES] * v)
                        .reshape(r // 8, 8, V7X_LANES), axis=0) for mp in range(2)]

    @pl.when(p == 0)
    def _():
        e_r = lax.broadcasted_iota(jnp.int32, ex_ref.shape, 0)
        e_c = lax.broadcasted_iota(jnp.int32, ex_ref.shape, 1)
        ex_ref[...] = jnp.where((e_r < nj) & (e_r % 2 == e_c // V7X_LANES), 1.0, 0.0).astype(BF16)
        qrows = jnp.zeros((8, V7X_LANES), F32)
        q = q_ref[0].astype(F32)
        for h in range(n_heads):
            qh = jnp.broadcast_to(q[:, h * V7X_LANES:(h + 1) * V7X_LANES], (8, V7X_LANES))
            qrows = jnp.where(sub // 2 == h, qh, qrows)
        qrows = jnp.where(lane // DA_QK == sub % 2, qrows, 0.0)
        qm_ref[...] = jnp.zeros(qm_ref.shape, BF16)
        qm_ref[0:8, :] = qrows.astype(BF16)
        kn = head_rows(kn_ref[0].astype(F32))
        s_new = lax.dot_general(kn.astype(BF16), qm_ref[...], dn, preferred_element_type=F32)
        m_g, pr, l_g = softmax_rows(s_new, live & (sub < n_heads))
        m_ref[...] = m_g
        l_ref[...] = l_g
        pb = jnp.dot(pr.astype(BF16), ex_ref[...], preferred_element_type=F32)
        parts = weighted_values(pb, head_rows(vn_ref[0].astype(F32)))
        for mp in range(2):
            acc_ref[mp] = parts[mp]

    qm = qm_ref[...]
    s_pg = [lax.dot_general(kp_refs[g][0].astype(BF16), qm, dn, preferred_element_type=F32)
            for g in range(gp)]
    stats = [softmax_rows(s, live) for s in s_pg]
    ex = ex_ref[...]
    pb_pg = [jnp.dot(st[1].astype(BF16), ex, preferred_element_type=F32) for st in stats]
    part_pg = [weighted_values(pb, vp_refs[g][0]) for g, pb in enumerate(pb_pg)]
    m_old = m_ref[...]
    m_new = functools.reduce(jnp.maximum, [st[0] for st in stats], m_old)
    c = jnp.exp2(m_old - m_new)
    w_pg = [jnp.exp2(st[0] - m_new) for st in stats]
    l_ref[...] = c * l_ref[...] + functools.reduce(
        jnp.add, [w * st[2] for w, st in zip(w_pg, stats)])
    m_ref[...] = m_new
    for mp in range(2):
        acc = per_row(c, mp) * acc_ref[mp]
        for g in range(gp):
            acc = acc + per_row(w_pg[g], mp) * part_pg[g][mp]
        acc_ref[mp] = acc

    @pl.when(p == n_steps - 1)
    def _():
        lam = _lambda(lq1_ref, lk1_ref, lq2_ref, lk2_ref, lam_init)
        on = [acc_ref[mp] / per_row(l_ref[...], mp) for mp in range(2)]
        for h in range(n_heads):
            o1, o2 = [functools.reduce(jnp.add, [x[i:i + 1, :] for i in range(h, 8, n_heads)])
                      for x in on]
            o_ref[0, :, h * V7X_LANES:(h + 1) * V7X_LANES] = _subln(
                o1 - lam * o2, g_ref[...], lam_init).astype(o_ref.dtype)


def _attn_decode(q, kb, vb, cache_k, cache_v, page_table, page_base, lq1, lk1, lq2, lk2, subln_g,
                 *, n_heads, gp, lam_init):
    db, width = q.shape
    n_pages = page_table.shape[1]
    rows = cache_k.shape[1]
    assert 8 % n_heads == 0 and rows % 8 == 0
    n_steps = n_pages // gp
    vec = lambda n: pl.BlockSpec((1, n), lambda bi, p, pt: (0, 0))
    tok = pl.BlockSpec((1, 1, width), lambda bi, p, pt: (bi, 0, 0))
    pages = [pl.BlockSpec((1, rows, V7X_LANES),
                          lambda bi, p, pt, g=g: (page_base + pt[bi, p * gp + g], 0, 0))
             for g in range(gp)]
    grid_spec = pltpu.PrefetchScalarGridSpec(
        num_scalar_prefetch=1,
        grid=(db, n_steps),
        in_specs=[vec(DA_QK), vec(DA_QK), vec(DA_QK), vec(DA_QK), vec(DA_V), tok, tok, tok]
                 + pages + pages,
        out_specs=tok,
        scratch_shapes=[pltpu.VMEM((V7X_LANES, V7X_LANES), BF16),
                        pltpu.VMEM((V7X_LANES, 2 * V7X_LANES), BF16),
                        pltpu.VMEM((1, V7X_LANES), F32), pltpu.VMEM((1, V7X_LANES), F32),
                        pltpu.VMEM((2, 8, V7X_LANES), F32)],
    )
    out = pl.pallas_call(
        functools.partial(_attn_decode_kernel, n_heads=n_heads, n_steps=n_steps, gp=gp,
                          lam_init=lam_init),
        grid_spec=grid_spec,
        out_shape=jax.ShapeDtypeStruct((db, 1, width), BF16),
        compiler_params=_cparams(("parallel", "arbitrary")),
        name="attn_decode",
    )(page_table, lq1, lk1, lq2, lk2, subln_g,
      q.reshape(db, 1, width), kb.reshape(db, 1, width), vb.reshape(db, 1, width),
      *([cache_k] * gp), *([cache_v] * gp))
    return out.reshape(db, width)


def _rwkv_prep_math(x, prev, mu, w0, w2p, a0, a2p, g2, k_k, k_a, r_k, ones_bd,
                    before_dots=None, after_dots=None):
    rd = ones_bd.shape[0]
    xm = x + (prev - x) * mu
    r = xm[:, 0:rd]
    k = xm[:, rd:2 * rd]
    v = xm[:, 2 * rd:3 * rd]
    xwa = xm[:, 3 * rd:3 * rd + RW_W_LORA + RW_A_LORA]
    xg = xm[:, 3 * rd + RW_W_LORA + RW_A_LORA:]
    tw = jnp.tanh(xwa)
    sg = jax.nn.sigmoid(xg)
    kk = k * k_k
    kk_sq = kk * kk
    if before_dots is not None:
        before_dots()
    lora_w = _dot(tw, w2p)
    lora_a = _dot(xwa, a2p)
    g = _dot(sg, g2)
    n2 = _dot(kk_sq, ones_bd)
    if after_dots is not None:
        after_dots()
    zw = -(w0 + lora_w)
    softplus = jnp.maximum(zw, 0.0) + jnp.log(1.0 + jnp.exp(-jnp.abs(zw)))
    logw = -jnp.exp(-softplus - 0.5)
    a = jax.nn.sigmoid(a0 + lora_a)
    kk = kk / jnp.maximum(jnp.sqrt(n2), 1e-12)
    k2 = k * (1.0 + (a - 1.0) * k_a)
    bonus = _dot(r * k2 * r_k, ones_bd) * v
    return r, k2, v, -kk, kk * a, logw, g, bonus


def _rwkv_weights(p, rd):
    zeros_w = jnp.zeros((RW_A_LORA, rd), F32)
    zeros_a = jnp.zeros((RW_W_LORA, rd), F32)
    return [p['rw_mu'][None, :], p['rw_w0'][None, :],
            jnp.concatenate([p['rw_w2'], zeros_w], axis=0).astype(BF16), p['rw_a0'][None, :],
            jnp.concatenate([zeros_a, p['rw_a2']], axis=0).astype(BF16), p['rw_g2'].astype(BF16),
            p['rw_k_k'][None, :], p['rw_k_a'][None, :], p['rw_r_k'].reshape(1, rd)]


def _weight_specs(weights, nargs):
    zero = {1: lambda i: (0, 0), 2: lambda i, j: (0, 0)}[nargs]
    return [pl.BlockSpec(w.shape, zero) for w in weights]


def _rwkv_masks():
    n = RW_NG
    row = np.arange(n)[:, None]
    col = np.arange(n)[None, :]
    same = (row // RW_CHUNK) == (col // RW_CHUNK)
    masks = [same & (col < row), same & (col <= row), (row // 2 == col // 2) & (col < row)]
    s = 2
    while s < RW_CHUNK:
        masks.append(((row // s) % 2 == 1) & ((col // s) == (row // s) - 1))
        s *= 2
    return np.stack(masks).astype(np.float32)


def _rwkv_chunk_chains(chains, masks_ref, tri, s_olds):
    c, ng = chains[0][0].shape
    lane = lax.broadcasted_iota(jnp.int32, (c, ng), 1)
    row = lax.broadcasted_iota(jnp.int32, (ng, ng), 0)
    col = lax.broadcasted_iota(jnp.int32, (ng, ng), 1)
    d = functools.partial(jnp.dot, preferred_element_type=F32)
    cat0 = lambda *xs: jnp.concatenate(xs, axis=0)
    cat1 = lambda *xs: jnp.concatenate(xs, axis=1)

    def stack(x):
        return cat0(*[jnp.where(lane // RW_HEAD == h, x, 0.0) for h in range(ng // RW_HEAD)])

    def split3(logw):
        hi = logw.astype(BF16)
        rem = logw - hi.astype(F32)
        mid = rem.astype(BF16)
        return hi, mid, (rem - mid.astype(F32)).astype(BF16)

    parts = [split3(ch[5]) for ch in chains]
    cums = [d(tri, hi) + d(tri, mid) + d(tri, lo) for hi, mid, lo in parts]

    def decayed(ch, cum):
        r, k2, v, na, b, logw = ch
        cum_c = cum[c - 1:c, :]
        g_inv = jnp.exp(-cum)
        g_tail = jnp.exp(cum_c - cum)
        return dict(at=stack(na * jnp.exp(cum - logw)), rt=stack(r * jnp.exp(cum)),
                    btkt=cat0(stack(b * g_inv), stack(k2 * g_inv)),
                    bhkh=cat0(stack(b * g_tail), stack(k2 * g_tail)), vm=stack(v),
                    gc=jnp.exp(cum_c))

    q = [decayed(ch, cum) for ch, cum in zip(chains, cums)]
    grams = [_dot_nt(cat0(x['at'], x['rt']), x['btkt']) for x in q]
    strict = masks_ref[0]
    incl = masks_ref[1]
    ns = [g[:ng, :ng] * strict for g in grams]
    gaks = [g[:ng, ng:] * strict for g in grams]
    grbs = [g[ng:, :ng] * incl for g in grams]
    grks = [g[ng:, ng:] * incl for g in grams]

    eye = jnp.where(row == col, 1.0, 0.0)
    xs = [eye + n * masks_ref[2] for n in ns]
    for lvl in range(3, masks_ref.shape[0]):
        lvl_mask = masks_ref[lvl]
        ys = [_dot(x, n * lvl_mask) for x, n in zip(xs, ns)]
        xs = [x + _dot(y, x) for x, y in zip(xs, ys)]

    gvs = [_dot(gak, x['vm']) for gak, x in zip(gaks, q)]
    aus = [_dot(x, cat1(y['at'], gv)) for x, y, gv in zip(xs, q, gvs)]
    ts = [_dot(grb, au) for grb, au in zip(grbs, aus)]
    gkvs = [_dot(grk, x['vm']) for grk, x in zip(grks, q)]
    rbs = [x['rt'] + t[:, :ng] for x, t in zip(q, ts)]
    obs = [t[:, ng:] + gkv for t, gkv in zip(ts, gkvs)]
    ms = [jnp.where(row == col, x['gc'], 0.0) + _dot(au[:, :ng].T, x['bhkh'][:ng])
          for x, au in zip(q, aus)]
    zs = [_dot(cat0(au[:, ng:], x['vm']).T, x['bhkh']) for x, au in zip(q, aus)]

    os_ = [_dot_nt(rb, s_old) + ob for rb, s_old, ob in zip(rbs, s_olds, obs)]
    s_news = [_dot(s_old, m) + z for s_old, m, z in zip(s_olds, ms, zs)]
    o_toks = [functools.reduce(jnp.add, [o[h * c:(h + 1) * c] for h in range(ng // RW_HEAD)])
              for o in os_]
    return o_toks, s_news


def _rwkv_chunk_kernel(r_ref, k_ref, v_ref, na_ref, b_ref, lw_ref, masks_ref, tri_ref,
                       o_ref, sout_ref, s_ref, *, bb, n_groups, n_chunks):
    ci = pl.program_id(1)

    @pl.when(ci == 0)
    def _():
        s_ref[...] = jnp.zeros_like(s_ref)

    ids = [(bi, gi) for bi in range(bb) for gi in range(n_groups)]
    sl = lambda gi: slice(gi * RW_NG, (gi + 1) * RW_NG)
    chains = [tuple(ref[bi, :, sl(gi)].astype(F32)
                    for ref in (r_ref, k_ref, v_ref, na_ref, b_ref, lw_ref))
              for bi, gi in ids]
    o_toks, s_news = _rwkv_chunk_chains(chains, masks_ref, tri_ref[...],
                                        [s_ref[bi * n_groups + gi] for bi, gi in ids])
    for (bi, gi), o_tok, s_new in zip(ids, o_toks, s_news):
        o_ref[bi, :, sl(gi)] = o_tok
        s_ref[bi * n_groups + gi] = s_new

    @pl.when(ci == n_chunks - 1)
    def _():
        for bi, gi in ids:
            s = s_ref[bi * n_groups + gi]
            for h in range(RW_GROUP):
                hs = slice(h * RW_HEAD, (h + 1) * RW_HEAD)
                sout_ref[bi, gi * RW_GROUP + h] = s[hs, hs]


def _rwkv_chunked(r, k2, v, na, b, logw, *, bb):
    bsz, s, rd = r.shape
    n_groups = rd // RW_NG
    n_chunks = s // RW_CHUNK
    n_heads = rd // RW_HEAD
    masks = jnp.asarray(_rwkv_masks())
    tri = jnp.asarray(np.tril(np.ones((RW_CHUNK, RW_CHUNK), np.float32))).astype(BF16)
    tok = pl.BlockSpec((bb, RW_CHUNK, rd), lambda bi, ci: (bi, ci, 0))
    return pl.pallas_call(
        functools.partial(_rwkv_chunk_kernel, bb=bb, n_groups=n_groups, n_chunks=n_chunks),
        grid=(bsz // bb, n_chunks),
        in_specs=[tok] * 6 + [pl.BlockSpec(masks.shape, lambda bi, ci: (0, 0, 0)),
                              pl.BlockSpec(tri.shape, lambda bi, ci: (0, 0))],
        out_specs=[tok, pl.BlockSpec((bb, n_heads, RW_HEAD, RW_HEAD), lambda bi, ci: (bi, 0, 0, 0))],
        out_shape=[jax.ShapeDtypeStruct((bsz, s, rd), F32),
                   jax.ShapeDtypeStruct((bsz, n_heads, RW_HEAD, RW_HEAD), F32)],
        scratch_shapes=[pltpu.VMEM((bb * n_groups, RW_NG, RW_NG), F32)],
        compiler_params=_cparams(("parallel", "arbitrary")),
        name="rwkv_chunked",
    )(r, k2, v, na, b, logw, masks, tri)


def _rwkv_step_kernel(s_ref, r_ref, k_ref, na_ref, b_ref, lw_ref, v_ref, sout_ref, o_ref):
    s = s_ref[...]
    sa = jnp.sum(s * na_ref[...], axis=-1, keepdims=True)
    s_new = s * jnp.exp(lw_ref[...]) + sa * b_ref[...] + v_ref[...] * k_ref[...]
    sout_ref[...] = s_new
    o_ref[...] = jnp.sum(s_new * r_ref[...], axis=-1, keepdims=True)


def _rwkv_step(state, r, k2, v, na, b, logw, *, bs):
    db, nh, hd, _ = state.shape
    assert db % bs == 0
    rowv = lambda x: x.reshape(db, nh, 1, hd)
    rspec = pl.BlockSpec((bs, nh, 1, hd), lambda i: (i, 0, 0, 0))
    cspec = pl.BlockSpec((bs, nh, hd, 1), lambda i: (i, 0, 0, 0))
    sspec = pl.BlockSpec((bs, nh, hd, hd), lambda i: (i, 0, 0, 0))
    s_new, o = pl.pallas_call(
        _rwkv_step_kernel,
        grid=(db // bs,),
        in_specs=[sspec, rspec, rspec, rspec, rspec, rspec, cspec],
        out_specs=[sspec, cspec],
        out_shape=[jax.ShapeDtypeStruct(state.shape, F32), jax.ShapeDtypeStruct((db, nh, hd, 1), F32)],
        compiler_params=_cparams(("parallel",)),
        name="rwkv_step",
    )(state, rowv(r), rowv(k2), rowv(na), rowv(b), rowv(logw), v.reshape(db, nh, hd, 1))
    return o.reshape(db, nh * hd), s_new


def _outproj_ln_kernel(oda_ref, orw_ref, bonus_ref, gate_ref, h_ref, w_ref, lg_ref, lb_ref,
                       ones_ref, g_ref, b_ref, o_ref, *, alpha):
    o = orw_ref[...]
    ones_bd = ones_ref[...]
    inv_n = 1.0 / RW_HEAD
    mu = _dot_exact_rhs(o, ones_bd) * inv_n
    d = o - mu
    var = _dot(d * d, ones_bd) * inv_n
    o = d * lax.rsqrt(var + RW_GN_EPS) * lg_ref[...] + lb_ref[...]
    o_rw = (o + bonus_ref[...]) * gate_ref[...]
    da = oda_ref.shape[1]
    mix = (jnp.dot(oda_ref[...], w_ref[0:da, :], preferred_element_type=F32)
           + jnp.dot(o_rw.astype(BF16), w_ref[da:, :], preferred_element_type=F32))
    o_ref[...] = _layernorm(alpha * h_ref[...] + mix, g_ref[...], b_ref[...])


def _outproj_ln(oda, orw, bonus, gate, h, w_out, lnx_g, lnx_b, ones_bd, g, b, *, alpha, tm):
    m, d = h.shape
    da = oda.shape[1]
    rd = orw.shape[1]
    tok = lambda n: pl.BlockSpec((tm, n), lambda i: (i, 0))
    consts = [w_out, lnx_g, lnx_b, ones_bd, g, b]
    return pl.pallas_call(
        functools.partial(_outproj_ln_kernel, alpha=alpha),
        grid=(m // tm,),
        in_specs=[tok(da), tok(rd), tok(rd), tok(rd), tok(d)] + _weight_specs(consts, 1),
        out_specs=tok(d),
        out_shape=jax.ShapeDtypeStruct((m, d), F32),
        compiler_params=_cparams(("parallel",)),
        name="outproj_ln",
    )(oda, orw, bonus, gate, h, *consts)


def _tile(m, pref):
    return pref if m % pref == 0 else m


def kernel(x_prompt, x_sample, cache_k, cache_v, state_wkv, state_shift, page_table, ffa_w_gu, ffa_w_down, ffb_w_gu, ffb_w_down, ln1_g, ln1_b, ln2_g, ln2_b, ln3_g, ln3_b, w_in, w_out, da_lq1, da_lk1, da_lq2, da_lk2, da_subln_g, rw_mu, rw_w0, rw_w2, rw_a0, rw_a2, rw_g2, rw_k_k, rw_k_a, rw_r_k, rw_lnx_g, rw_lnx_b):
    bsz, seq, d_model = x_prompt.shape
    db, dec_seq, _ = x_sample.shape
    assert dec_seq == 1, "the sample group is written for one new token per sequence"
    depth = w_in.shape[0]
    n_pool, page, n_heads, _ = cache_k.shape[1:]
    n_pages = page_table.shape[1]
    rd = rw_w0.shape[1]
    rw_cols = rw_mu.shape[1]
    rw_heads = rd // RW_HEAD
    d_ff = ffa_w_down.shape[1]
    alpha = (2 * depth) ** 0.25
    width = n_heads * V7X_LANES

    cos_p, sin_p = _rope_tables(jnp.arange(seq))
    cos_s, sin_s = _rope_tables(jnp.full((db,), n_pages * page, jnp.int32))
    ones_bd = jnp.asarray(np.kron(np.eye(rw_heads, dtype=np.float32),
                                  np.ones((RW_HEAD, RW_HEAD), np.float32))).astype(BF16)

    m_p = bsz * seq
    tm_p = _tile(m_p, TOKEN_TILE)
    tf = _tile(d_ff, FFN_CHUNK)
    y_p = x_prompt.reshape(m_p, d_model)
    y_s = x_sample.reshape(db, d_model)
    outs = [[] for _ in range(8)]
    row = lambda a: a[None, :]

    for l in range(depth):
        lam_init = 0.8 - 0.6 * math.exp(-0.3 * l)
        wa_gu, wa_dn = ffa_w_gu[l].astype(BF16), ffa_w_down[l].astype(BF16)
        wb_gu, wb_dn = ffb_w_gu[l].astype(BF16), ffb_w_down[l].astype(BF16)
        wi, wo = w_in[l].astype(BF16), w_out[l].astype(BF16)
        lam_vecs = [row(da_lq1[l]), row(da_lk1[l]), row(da_lq2[l]), row(da_lk2[l]), row(da_subln_g[l])]
        rw_weights = _rwkv_weights(dict(rw_mu=rw_mu[l], rw_w0=rw_w0[l], rw_w2=rw_w2[l], rw_a0=rw_a0[l],
                                        rw_a2=rw_a2[l], rw_g2=rw_g2[l], rw_k_k=rw_k_k[l],
                                        rw_k_a=rw_k_a[l], rw_r_k=rw_r_k[l]), rd)
        rw_consts = rw_weights + [ones_bd]
        ln = [row(a[l]) for a in (ln1_g, ln1_b, ln2_g, ln2_b, ln3_g, ln3_b)]
        lnx = [row(rw_lnx_g[l]), row(rw_lnx_b[l])]

        h1 = _ffn_ln(y_p, wa_gu, wa_dn, ln[0], ln[1], alpha=alpha, tm=tm_p, tf=tf)
        shift0 = jnp.zeros((bsz, 1, rw_cols), F32)
        q, k, kb, v, vb, shift_p, r_, k2_, v_, na_, b_, lw_, gate, bonus = _inproj(
            h1, wi, cos_p, sin_p, shift0, rw_consts, n_heads=n_heads, tm=tm_p, seq_len=seq,
            rw_dtype=BF16)
        oda = _attn_prompt(q.reshape(bsz, seq, width), kb.reshape(bsz, seq, width),
                           vb.reshape(bsz, seq, width), *lam_vecs,
                           n_heads=n_heads, tq=_tile(seq, ATTN_BLOCK), rs=ATTN_ROW_CHUNK,
                           lam_init=lam_init)
        seq3 = lambda a: a.reshape(bsz, seq, rd)
        orw, wkv_p = _rwkv_chunked(seq3(r_), seq3(k2_), seq3(v_), seq3(na_), seq3(b_), seq3(lw_),
                                   bb=_tile(bsz, RW_SEQS_PER_STEP))
        h2 = _outproj_ln(oda.reshape(m_p, width), orw.reshape(m_p, rd), bonus, gate, h1, wo, *lnx,
                         ones_bd, ln[2], ln[3], alpha=alpha, tm=tm_p)
        y_p = _ffn_ln(h2, wb_gu, wb_dn, ln[4], ln[5], alpha=alpha, tm=tm_p, tf=tf)
        outs[0].append(k.reshape(bsz, seq, n_heads, 2 * DA_QK))
        outs[1].append(v.reshape(bsz, seq, n_heads, DA_V))
        outs[2].append(wkv_p)
        outs[3].append(shift_p)

        h1 = _ffn_ln(y_s, wa_gu, wa_dn, ln[0], ln[1], alpha=alpha, tm=db, tf=tf)
        q, k, kb, v, vb, shift_s, r_, k2_, v_, na_, b_, lw_, gate, bonus = _inproj(
            h1, wi, cos_s, sin_s, state_shift[l].reshape(db, rw_cols), rw_consts, n_heads=n_heads,
            tm=db, seq_len=None, rw_dtype=F32)
        oda = _attn_decode(q, kb, vb, cache_k.reshape(depth * n_pool, page * n_heads, V7X_LANES),
                           cache_v.reshape(depth * n_pool, page * n_heads, DA_V), page_table, l * n_pool,
                           *lam_vecs, n_heads=n_heads, gp=_tile(n_pages, DECODE_PAGES_PER_STEP),
                           lam_init=lam_init)
        orw, wkv_s = _rwkv_step(state_wkv[l], r_, k2_, v_, na_, b_, lw_,
                                bs=_tile(db, RW_STEP_SEQS))
        h2 = _outproj_ln(oda, orw, bonus, gate, h1, wo, *lnx, ones_bd, ln[2], ln[3],
                         alpha=alpha, tm=db)
        y_s = _ffn_ln(h2, wb_gu, wb_dn, ln[4], ln[5], alpha=alpha, tm=db, tf=tf)
        outs[4].append(k.reshape(db, 1, n_heads, 2 * DA_QK))
        outs[5].append(v.reshape(db, 1, n_heads, DA_V))
        outs[6].append(wkv_s)
        outs[7].append(shift_s.reshape(db, 1, rw_cols))

    stacked = [jnp.stack(o) for o in outs]
    return (y_p.reshape(bsz, seq, d_model), y_s.reshape(db, 1, d_model), *stacked)
```

```python
import functools
import math

import jax
import jax.numpy as jnp
import numpy as np
from jax import lax
from jax.experimental import pallas as pl
from jax.experimental.pallas import tpu as pltpu

F32 = jnp.float32
BF16 = jnp.bfloat16

DA_QK = 64
DA_V = 2 * DA_QK
RW_HEAD = 64
RW_W_LORA = 64
RW_A_LORA = 64
RW_G_LORA = 128
RW_GN_EPS = 64e-5
ROPE_THETA = 10000.0
LN_EPS = 1e-5
NEG = -1e30
LOG2E = math.log2(math.e)

V7X_LANES = 128
V7X_VMEM_LIMIT_BYTES = 56 * 1024 * 1024

RW_CHUNK = 64
RW_GROUP = 4
RW_NG = RW_CHUNK * RW_GROUP

TOKEN_TILE = 512
FFN_CHUNK = 256
ATTN_BLOCK = 512
ATTN_ROW_CHUNK = 128
DECODE_PAGES_PER_STEP = 32
RW_SEQS_PER_STEP = 4
RW_STEP_SEQS = 8


def _cparams(semantics):
    return pltpu.CompilerParams(dimension_semantics=semantics,
                                vmem_limit_bytes=V7X_VMEM_LIMIT_BYTES)


def _dot(a, b):
    return jnp.dot(a.astype(BF16), b.astype(BF16), preferred_element_type=F32)


def _dot_nt(a, b):
    return lax.dot_general(a.astype(BF16), b.astype(BF16), (((1,), (1,)), ((), ())),
                           preferred_element_type=F32)


def _split(x):
    hi = x.astype(BF16)
    lo = (x - hi.astype(F32)).astype(BF16)
    return hi, lo


def _dot_exact_rhs(a, b):
    ah, al = _split(a)
    return (jnp.dot(ah, b, preferred_element_type=F32)
            + jnp.dot(al, b, preferred_element_type=F32))


def _layernorm(y, g, b):
    mu = jnp.mean(y, axis=-1, keepdims=True)
    d = y - mu
    var = jnp.mean(d * d, axis=-1, keepdims=True)
    return d * lax.rsqrt(var + LN_EPS) * g + b


def _ffn_ln_kernel(x_ref, wgu_ref, wd_ref, g_ref, b_ref, o_ref, *, d_ff, tf, alpha):
    x = x_ref[...]
    xb = x.astype(BF16)

    def activation(c):
        gate = jnp.dot(xb, wgu_ref[:, c * tf:(c + 1) * tf], preferred_element_type=F32)
        up = jnp.dot(xb, wgu_ref[:, d_ff + c * tf:d_ff + (c + 1) * tf], preferred_element_type=F32)
        return (gate * jax.nn.sigmoid(gate) * up).astype(BF16)

    nf = d_ff // tf
    act = activation(0)
    acc = None
    for c in range(nf):
        nxt = activation(c + 1) if c + 1 < nf else None
        part = jnp.dot(act, wd_ref[c * tf:(c + 1) * tf, :], preferred_element_type=F32)
        acc = part if acc is None else acc + part
        act = nxt
    o_ref[...] = _layernorm(alpha * x + 0.5 * acc, g_ref[...], b_ref[...])


def _ffn_ln(x, w_gu, w_down, g, b, *, alpha, tm, tf):
    m, d = x.shape
    f = w_down.shape[0]
    assert m % tm == 0 and f % tf == 0
    const = lambda shape: pl.BlockSpec(shape, lambda i: (0, 0))
    return pl.pallas_call(
        functools.partial(_ffn_ln_kernel, d_ff=f, tf=tf, alpha=alpha),
        grid=(m // tm,),
        in_specs=[pl.BlockSpec((tm, d), lambda i: (i, 0)), const(w_gu.shape), const(w_down.shape),
                  const((1, d)), const((1, d))],
        out_specs=pl.BlockSpec((tm, d), lambda i: (i, 0)),
        out_shape=jax.ShapeDtypeStruct((m, d), F32),
        compiler_params=_cparams(("parallel",)),
        name="ffn_ln",
    )(x, w_gu, w_down, g, b)


def _inproj_kernel(h_ref, w_ref, cos_ref, sin_ref, prev_ref, mu_ref, w0_ref, w2p_ref, a0_ref,
                   a2p_ref, g2_ref, kk_ref, ka_ref, rk_ref, ones_ref,
                   q_ref, k_ref, kb_ref, v_ref, vb_ref, shift_ref, *rest,
                   n_heads, scale, steps_per_seq):
    rw_outs, carry_ref, xs_ref = rest[:-2], rest[-2], rest[-1]
    hb = h_ref[...].astype(BF16)
    tm = hb.shape[0]
    cos = cos_ref[...]
    sin = sin_ref[...]
    lane = lax.broadcasted_iota(jnp.int32, cos.shape, 1)
    first_half = (lane % DA_QK) < (DA_QK // 2)
    qc = n_heads * 2 * DA_QK

    def rope_head(x):
        rot = jnp.where(first_half, pltpu.roll(x, V7X_LANES - DA_QK // 2, 1),
                        pltpu.roll(x, DA_QK // 2, 1))
        return x * cos + rot * sin

    def project_q():
        q = jnp.dot(hb, w_ref[:, 0:qc], preferred_element_type=F32)
        for h in range(n_heads):
            sl = slice(h * V7X_LANES, (h + 1) * V7X_LANES)
            q_ref[:, sl] = (rope_head(q[:, sl]) * scale).astype(BF16)

    def project_kv():
        k = jnp.dot(hb, w_ref[:, qc:2 * qc], preferred_element_type=F32)
        v = jnp.dot(hb, w_ref[:, 2 * qc:2 * qc + vc], preferred_element_type=F32)
        for h in range(n_heads):
            kh = rope_head(k[:, h * V7X_LANES:(h + 1) * V7X_LANES])
            kb_ref[:, h * V7X_LANES:(h + 1) * V7X_LANES] = kh.astype(BF16)
            k_ref[pl.ds(h, tm, stride=n_heads), :] = kh
            v_ref[pl.ds(h, tm, stride=n_heads), :] = v[:, h * DA_V:(h + 1) * DA_V]
        vb_ref[...] = v.astype(BF16)

    vc = n_heads * DA_V
    def project_rw():
        return jnp.dot(hb, w_ref[:, 2 * qc + vc:], preferred_element_type=F32)

    if steps_per_seq is None:
        x = project_rw()
        prev = prev_ref[...]
        shift_ref[...] = x
        before_dots, after_dots = project_q, project_kv
    else:
        i = pl.program_id(0)

        @pl.when(i == 0)
        def _():
            xs_ref[...] = jnp.zeros(xs_ref.shape, F32)
            carry_ref[...] = jnp.zeros(carry_ref.shape, F32)

        @pl.when((i - 1) % steps_per_seq == 0)
        def _():
            carry_ref[...] = prev_ref[0]

        x = xs_ref[...]
        row = lax.broadcasted_iota(jnp.int32, x.shape, 0)
        prev = jnp.where(row == 0, carry_ref[...], pltpu.roll(x, 1, 0))
        carry_ref[...] = x[tm - 1:tm, :]
        shift_ref[0] = x[tm - 1:tm, :]
        parked = []

        before_dots = None

        def after_dots():
            parked.append(project_rw())
            project_q()
            project_kv()

    vals = _rwkv_prep_math(x, prev, mu_ref[...], w0_ref[...], w2p_ref[...], a0_ref[...],
                           a2p_ref[...], g2_ref[...], kk_ref[...], ka_ref[...], rk_ref[...],
                           ones_ref[...], before_dots=before_dots, after_dots=after_dots)
    for o_ref, val in zip(rw_outs, vals):
        o_ref[...] = val.astype(o_ref.dtype)
    if steps_per_seq is not None:
        xs_ref[...] = parked[0]


def _inproj(h, w_in, cos_tab, sin_tab, prev, rw_consts, *, n_heads, tm, seq_len, rw_dtype):
    m, d = h.shape
    cols = w_in.shape[1]
    qc = n_heads * 2 * DA_QK
    vc = n_heads * DA_V
    rw = cols - 2 * qc - vc
    rd = rw_consts[-1].shape[0]
    n_pos_blocks = cos_tab.shape[0] // tm
    n = m // tm
    if seq_len is None:
        steps_per_seq = None
        n_steps = n
        tok = lambda i: (i, 0)
        lag = tok
        prev_spec = pl.BlockSpec((tm, rw), tok)
        shift_spec, shift_shape = pl.BlockSpec((tm, rw), tok), (m, rw)
    else:
        assert seq_len % tm == 0
        steps_per_seq = seq_len // tm
        n_steps = n + 1
        tok = lambda i: (jnp.minimum(i, n - 1), 0)
        lag = lambda i: (jnp.maximum(i - 1, 0), 0)
        prev_spec = pl.BlockSpec((1, 1, rw),
                                 lambda i: (jnp.maximum(i - 1, 0) // steps_per_seq, 0, 0))
        shift_spec, shift_shape = prev_spec, (m // seq_len, 1, rw)
    pos = lambda i: (tok(i)[0] % n_pos_blocks, 0)
    rw_dtypes = [F32 if i == 5 else rw_dtype for i in range(8)]
    return pl.pallas_call(
        functools.partial(_inproj_kernel, n_heads=n_heads, scale=DA_QK ** -0.5 * LOG2E,
                          steps_per_seq=steps_per_seq),
        grid=(n_steps,),
        in_specs=[
            pl.BlockSpec((tm, d), tok),
            pl.BlockSpec((d, cols), lambda i: (0, 0)),
            pl.BlockSpec((tm, V7X_LANES), pos),
            pl.BlockSpec((tm, V7X_LANES), pos),
            prev_spec,
        ] + _weight_specs(rw_consts, 1),
        out_specs=[
            pl.BlockSpec((tm, qc), tok), pl.BlockSpec((tm * n_heads, 2 * DA_QK), tok),
            pl.BlockSpec((tm, qc), tok), pl.BlockSpec((tm * n_heads, DA_V), tok),
            pl.BlockSpec((tm, vc), tok), shift_spec,
        ] + [pl.BlockSpec((tm, rd), lag)] * 8,
        out_shape=[
            jax.ShapeDtypeStruct((m, qc), BF16), jax.ShapeDtypeStruct((m * n_heads, 2 * DA_QK), F32),
            jax.ShapeDtypeStruct((m, qc), BF16), jax.ShapeDtypeStruct((m * n_heads, DA_V), F32),
            jax.ShapeDtypeStruct((m, vc), BF16), jax.ShapeDtypeStruct(shift_shape, F32),
        ] + [jax.ShapeDtypeStruct((m, rd), dt) for dt in rw_dtypes],
        scratch_shapes=[pltpu.VMEM((1, rw), F32), pltpu.VMEM((tm, rw), F32)],
        compiler_params=_cparams(("arbitrary",)),
        name="inproj_rope",
    )(h, w_in, cos_tab, sin_tab, prev, *rw_consts)


def _rope_tables(positions):
    half = DA_QK // 2
    inv = ROPE_THETA ** (-jnp.arange(half, dtype=F32) / half)
    ang = positions.astype(F32)[:, None] * inv[None, :]
    cos = jnp.cos(ang)
    sin = jnp.sin(ang)
    cos_tab = jnp.concatenate([cos, cos, cos, cos], axis=1)
    sin_tab = jnp.concatenate([-sin, sin, -sin, sin], axis=1)
    return cos_tab, sin_tab


def _lambda(lq1_ref, lk1_ref, lq2_ref, lk2_ref, lam_init):
    s1 = jnp.sum(lq1_ref[...] * lk1_ref[...], axis=-1, keepdims=True)
    s2 = jnp.sum(lq2_ref[...] * lk2_ref[...], axis=-1, keepdims=True)
    return jnp.exp(s1) - jnp.exp(s2) + lam_init


def _subln(o, g, lam_init):
    return o * lax.rsqrt(jnp.mean(o * o, axis=-1, keepdims=True) + LN_EPS) * g * (1.0 - lam_init)


def _attn_prompt_kernel(lq1_ref, lk1_ref, lq2_ref, lk2_ref, g_ref, q_ref, k_ref, v_ref, o_ref,
                        qm_ref, vx_ref, s_ref, p_ref, m_ref, acc_ref, *, tq, rs, lam_init):
    i = pl.program_id(2)

    @pl.when(i == 0)
    def _():
        vx_ref[:, 0:DA_V] = v_ref[0]
        vx_ref[:, DA_V:] = jnp.ones((vx_ref.shape[0], V7X_LANES), BF16)

    q = q_ref[0].astype(F32)
    lane = lax.broadcasted_iota(jnp.int32, q.shape, 1)
    qm_ref[0] = jnp.where(lane < DA_QK, q, 0.0).astype(BF16)
    qm_ref[1] = jnp.where(lane >= DA_QK, q, 0.0).astype(BF16)
    m_ref[...] = jnp.full(m_ref.shape, NEG, F32)
    acc_ref[...] = jnp.zeros(acc_ref.shape, F32)
    nt = tq // V7X_LANES

    kw = 2 * V7X_LANES

    def score_parts(j, buf):
        def part(mp, t):
            kblk = k_ref[0, pl.ds(pl.multiple_of(j * tq + t * kw, kw), kw), :]
            s_ref[buf, mp, :, t * kw:(t + 1) * kw] = lax.dot_general(
                qm_ref[mp], kblk, (((1,), (1,)), ((), ())), preferred_element_type=F32)
        return [functools.partial(part, mp, t) for t in range(tq // kw) for mp in range(2)]

    def scores(j, buf):
        for part in score_parts(j, buf):
            part()

    def softmax_pv(j, buf, diagonal, fillers=()):
        off = pl.multiple_of(j * tq, tq)
        fillers = list(fillers)
        for r in range(tq // rs):
            if fillers:
                fillers.pop(0)()
            rows = slice(r * rs, (r + 1) * rs)
            n_vis = min(nt, pl.cdiv((r + 1) * rs, V7X_LANES)) if diagonal else nt
            vblk = vx_ref[pl.ds(off, n_vis * V7X_LANES), :]
            def tile(mp, t):
                s = s_ref[buf, mp, rows, t * V7X_LANES:(t + 1) * V7X_LANES]
                if diagonal and (t + 1) * V7X_LANES > r * rs + 1:
                    row = lax.broadcasted_iota(jnp.int32, s.shape, 0) + r * rs
                    col = lax.broadcasted_iota(jnp.int32, s.shape, 1) + t * V7X_LANES
                    s = jnp.where(col <= row, s, NEG)
                return s

            for mp in range(2):
                mx = functools.reduce(jnp.maximum, [tile(mp, t) for t in range(n_vis)])
                m_old = m_ref[mp, rows, :]
                m_new = jnp.maximum(m_old, jnp.max(mx, axis=1, keepdims=True))
                for t in range(n_vis):
                    p_ref[mp, rows, t * V7X_LANES:(t + 1) * V7X_LANES] = jnp.exp2(
                        tile(mp, t) - m_new).astype(BF16)
                c = jnp.exp2(m_old - m_new)
                m_ref[mp, rows, :] = m_new
                pv = jnp.dot(p_ref[mp, rows, 0:n_vis * V7X_LANES], vblk, preferred_element_type=F32)
                acc_ref[mp, rows, :] = jnp.concatenate([c, c], axis=1) * acc_ref[mp, rows, :] + pv
        for part in fillers:
            part()

    scores(0, 0)

    def body(jj, carry):
        j = 2 * jj
        softmax_pv(j, 0, False, score_parts(j + 1, 1))
        softmax_pv(j + 1, 1, False, score_parts(j + 2, 0))
        return carry

    lax.fori_loop(0, i // 2, body, 0)

    @pl.when(i % 2 == 0)
    def _():
        softmax_pv(i, 0, True)

    @pl.when(i % 2 == 1)
    def _():
        softmax_pv(i - 1, 0, False, score_parts(i, 1))
        softmax_pv(i, 1, True)

    lam = _lambda(lq1_ref, lk1_ref, lq2_ref, lk2_ref, lam_init)
    o = (acc_ref[0, :, 0:DA_V] / acc_ref[0, :, DA_V:]
         - lam * (acc_ref[1, :, 0:DA_V] / acc_ref[1, :, DA_V:]))
    o_ref[0] = _subln(o, g_ref[...], lam_init).astype(o_ref.dtype)


def _attn_prompt(q, kb, vb, lq1, lk1, lq2, lk2, subln_g, *, n_heads, tq, rs, lam_init):
    b, s, _ = q.shape
    vec = lambda n: pl.BlockSpec((1, n), lambda bi, h, i: (0, 0))
    return pl.pallas_call(
        functools.partial(_attn_prompt_kernel, tq=tq, rs=rs, lam_init=lam_init),
        grid=(b, n_heads, s // tq),
        in_specs=[
            vec(DA_QK), vec(DA_QK), vec(DA_QK), vec(DA_QK), vec(DA_V),
            pl.BlockSpec((1, tq, V7X_LANES), lambda bi, h, i: (bi, i, h)),
            pl.BlockSpec((1, s, V7X_LANES), lambda bi, h, i: (bi, 0, h)),
            pl.BlockSpec((1, s, DA_V), lambda bi, h, i: (bi, 0, h)),
        ],
        out_specs=pl.BlockSpec((1, tq, DA_V), lambda bi, h, i: (bi, i, h)),
        out_shape=jax.ShapeDtypeStruct((b, s, n_heads * DA_V), BF16),
        scratch_shapes=[pltpu.VMEM((2, tq, V7X_LANES), BF16), pltpu.VMEM((s, DA_V + V7X_LANES), BF16),
                        pltpu.VMEM((2, 2, tq, tq), F32), pltpu.VMEM((2, tq, tq), BF16),
                        pltpu.VMEM((2, tq, V7X_LANES), F32),
                        pltpu.VMEM((2, tq, DA_V + V7X_LANES), F32)],
        compiler_params=_cparams(("parallel", "parallel", "arbitrary")),
        name="attn_prompt",
    )(lq1, lk1, lq2, lk2, subln_g, q, kb, vb)


def _attn_decode_kernel(pt_ref, lq1_ref, lk1_ref, lq2_ref, lk2_ref, g_ref, q_ref, kn_ref, vn_ref,
                        *rest, n_heads, n_steps, gp, lam_init):
    del pt_ref
    kp_refs, vp_refs = rest[:gp], rest[gp:2 * gp]
    o_ref, qm_ref, ex_ref, m_ref, l_ref, acc_ref = rest[2 * gp:]
    p = pl.program_id(1)
    nj = 2 * n_heads
    dn = (((1,), (1,)), ((), ()))
    sub = lax.broadcasted_iota(jnp.int32, (8, V7X_LANES), 0)
    lane = lax.broadcasted_iota(jnp.int32, (8, V7X_LANES), 1)
    live = (sub % n_heads) == (lane // 2)

    def head_rows(x):
        out = jnp.zeros((8, V7X_LANES), F32)
        for h in range(n_heads):
            xh = jnp.broadcast_to(x[:, h * V7X_LANES:(h + 1) * V7X_LANES], (8, V7X_LANES))
            out = jnp.where(sub == h, xh, out)
        return out

    def per_row(x, mp):
        pick = lane == 2 * (sub % n_heads) + mp
        return jnp.broadcast_to(jnp.sum(jnp.where(pick, x, 0.0), axis=1, keepdims=True),
                                (8, V7X_LANES))

    def softmax_rows(s, rows_live):
        r = s.shape[0]
        s3 = jnp.where(rows_live, s.reshape(r // 8, 8, V7X_LANES), NEG)
        m_g = jnp.max(jnp.max(s3, axis=0), axis=0, keepdims=True)
        pr = jnp.exp2(s3 - m_g).reshape(r, V7X_LANES)
        return m_g, pr, jnp.sum(pr, axis=0, keepdims=True)

    def weighted_values(pb, v):
        r = v.shape[0]
        return [jnp.sum((pb[:, mp * V7X_LANES:(mp + 1) * V7X_LANES] * v)
                        .reshape(r // 8, 8, V7X_LANES), axis=0) for mp in range(2)]

    @pl.when(p == 0)
    def _():
        e_r = lax.broadcasted_iota(jnp.int32, ex_ref.shape, 0)
        e_c = lax.broadcasted_iota(jnp.int32, ex_ref.shape, 1)
        ex_ref[...] = jnp.where((e_r < nj) & (e_r % 2 == e_c // V7X_LANES), 1.0, 0.0).astype(BF16)
        qrows = jnp.zeros((8, V7X_LANES), F32)
        q = q_ref[0].astype(F32)
        for h in range(n_heads):
            qh = jnp.broadcast_to(q[:, h * V7X_LANES:(h + 1) * V7X_LANES], (8, V7X_LANES))
            qrows = jnp.where(sub // 2 == h, qh, qrows)
        qrows = jnp.where(lane // DA_QK == sub % 2, qrows, 0.0)
        qm_ref[...] = jnp.zeros(qm_ref.shape, BF16)
        qm_ref[0:8, :] = qrows.astype(BF16)
        kn = head_rows(kn_ref[0].astype(F32))
        s_new = lax.dot_general(kn.astype(BF16), qm_ref[...], dn, preferred_element_type=F32)
        m_g, pr, l_g = softmax_rows(s_new, live & (sub < n_heads))
        m_ref[...] = m_g
        l_ref[...] = l_g
        pb = jnp.dot(pr.astype(BF16), ex_ref[...], preferred_element_type=F32)
        parts = weighted_values(pb, head_rows(vn_ref[0].astype(F32)))
        for mp in range(2):
            acc_ref[mp] = parts[mp]

    qm = qm_ref[...]
    s_pg = [lax.dot_general(kp_refs[g][0].astype(BF16), qm, dn, preferred_element_type=F32)
            for g in range(gp)]
    stats = [softmax_rows(s, live) for s in s_pg]
    ex = ex_ref[...]
    pb_pg = [jnp.dot(st[1].astype(BF16), ex, preferred_element_type=F32) for st in stats]
    part_pg = [weighted_values(pb, vp_refs[g][0]) for g, pb in enumerate(pb_pg)]
    m_old = m_ref[...]
    m_new = functools.reduce(jnp.maximum, [st[0] for st in stats], m_old)
    c = jnp.exp2(m_old - m_new)
    w_pg = [jnp.exp2(st[0] - m_new) for st in stats]
    l_ref[...] = c * l_ref[...] + functools.reduce(
        jnp.add, [w * st[2] for w, st in zip(w_pg, stats)])
    m_ref[...] = m_new
    for mp in range(2):
        acc = per_row(c, mp) * acc_ref[mp]
        for g in range(gp):
            acc = acc + per_row(w_pg[g], mp) * part_pg[g][mp]
        acc_ref[mp] = acc

    @pl.when(p == n_steps - 1)
    def _():
        lam = _lambda(lq1_ref, lk1_ref, lq2_ref, lk2_ref, lam_init)
        on = [acc_ref[mp] / per_row(l_ref[...], mp) for mp in range(2)]
        for h in range(n_heads):
            o1, o2 = [functools.reduce(jnp.add, [x[i:i + 1, :] for i in range(h, 8, n_heads)])
                      for x in on]
            o_ref[0, :, h * V7X_LANES:(h + 1) * V7X_LANES] = _subln(
                o1 - lam * o2, g_ref[...], lam_init).astype(o_ref.dtype)


def _attn_decode(q, kb, vb, cache_k, cache_v, page_table, page_base, lq1, lk1, lq2, lk2, subln_g,
                 *, n_heads, gp, lam_init):
    db, width = q.shape
    n_pages = page_table.shape[1]
    rows = cache_k.shape[1]
    assert 8 % n_heads == 0 and rows % 8 == 0
    n_steps = n_pages // gp
    vec = lambda n: pl.BlockSpec((1, n), lambda bi, p, pt: (0, 0))
    tok = pl.BlockSpec((1, 1, width), lambda bi, p, pt: (bi, 0, 0))
    pages = [pl.BlockSpec((1, rows, V7X_LANES),
                          lambda bi, p, pt, g=g: (page_base + pt[bi, p * gp + g], 0, 0))
             for g in range(gp)]
    grid_spec = pltpu.PrefetchScalarGridSpec(
        num_scalar_prefetch=1,
        grid=(db, n_steps),
        in_specs=[vec(DA_QK), vec(DA_QK), vec(DA_QK), vec(DA_QK), vec(DA_V), tok, tok, tok]
                 + pages + pages,
        out_specs=tok,
        scratch_shapes=[pltpu.VMEM((V7X_LANES, V7X_LANES), BF16),
                        pltpu.VMEM((V7X_LANES, 2 * V7X_LANES), BF16),
                        pltpu.VMEM((1, V7X_LANES), F32), pltpu.VMEM((1, V7X_LANES), F32),
                        pltpu.VMEM((2, 8, V7X_LANES), F32)],
    )
    out = pl.pallas_call(
        functools.partial(_attn_decode_kernel, n_heads=n_heads, n_steps=n_steps, gp=gp,
                          lam_init=lam_init),
        grid_spec=grid_spec,
        out_shape=jax.ShapeDtypeStruct((db, 1, width), BF16),
        compiler_params=_cparams(("parallel", "arbitrary")),
        name="attn_decode",
    )(page_table, lq1, lk1, lq2, lk2, subln_g,
      q.reshape(db, 1, width), kb.reshape(db, 1, width), vb.reshape(db, 1, width),
      *([cache_k] * gp), *([cache_v] * gp))
    return out.reshape(db, width)


def _rwkv_prep_math(x, prev, mu, w0, w2p, a0, a2p, g2, k_k, k_a, r_k, ones_bd,
                    before_dots=None, after_dots=None):
    rd = ones_bd.shape[0]
    xm = x + (prev - x) * mu
    r = xm[:, 0:rd]
    k = xm[:, rd:2 * rd]
    v = xm[:, 2 * rd:3 * rd]
    xwa = xm[:, 3 * rd:3 * rd + RW_W_LORA + RW_A_LORA]
    xg = xm[:, 3 * rd + RW_W_LORA + RW_A_LORA:]
    tw = jnp.tanh(xwa)
    sg = jax.nn.sigmoid(xg)
    kk = k * k_k
    kk_sq = kk * kk
    if before_dots is not None:
        before_dots()
    lora_w = _dot(tw, w2p)
    lora_a = _dot(xwa, a2p)
    g = _dot(sg, g2)
    n2 = _dot(kk_sq, ones_bd)
    if after_dots is not None:
        after_dots()
    zw = -(w0 + lora_w)
    softplus = jnp.maximum(zw, 0.0) + jnp.log(1.0 + jnp.exp(-jnp.abs(zw)))
    logw = -jnp.exp(-softplus - 0.5)
    a = jax.nn.sigmoid(a0 + lora_a)
    kk = kk / jnp.maximum(jnp.sqrt(n2), 1e-12)
    k2 = k * (1.0 + (a - 1.0) * k_a)
    bonus = _dot(r * k2 * r_k, ones_bd) * v
    return r, k2, v, -kk, kk * a, logw, g, bonus


def _rwkv_weights(p, rd):
    zeros_w = jnp.zeros((RW_A_LORA, rd), F32)
    zeros_a = jnp.zeros((RW_W_LORA, rd), F32)
    return [p['rw_mu'][None, :], p['rw_w0'][None, :],
            jnp.concatenate([p['rw_w2'], zeros_w], axis=0).astype(BF16), p['rw_a0'][None, :],
            jnp.concatenate([zeros_a, p['rw_a2']], axis=0).astype(BF16), p['rw_g2'].astype(BF16),
            p['rw_k_k'][None, :], p['rw_k_a'][None, :], p['rw_r_k'].reshape(1, rd)]


def _weight_specs(weights, nargs):
    zero = {1: lambda i: (0, 0), 2: lambda i, j: (0, 0)}[nargs]
    return [pl.BlockSpec(w.shape, zero) for w in weights]


def _rwkv_masks():
    n = RW_NG
    row = np.arange(n)[:, None]
    col = np.arange(n)[None, :]
    same = (row // RW_CHUNK) == (col // RW_CHUNK)
    masks = [same & (col < row), same & (col <= row), (row // 2 == col // 2) & (col < row)]
    s = 2
    while s < RW_CHUNK:
        masks.append(((row // s) % 2 == 1) & ((col // s) == (row // s) - 1))
        s *= 2
    return np.stack(masks).astype(np.float32)


def _rwkv_chunk_chains(chains, masks_ref, tri, s_olds):
    c, ng = chains[0][0].shape
    lane = lax.broadcasted_iota(jnp.int32, (c, ng), 1)
    row = lax.broadcasted_iota(jnp.int32, (ng, ng), 0)
    col = lax.broadcasted_iota(jnp.int32, (ng, ng), 1)
    d = functools.partial(jnp.dot, preferred_element_type=F32)
    cat0 = lambda *xs: jnp.concatenate(xs, axis=0)
    cat1 = lambda *xs: jnp.concatenate(xs, axis=1)

    def stack(x):
        return cat0(*[jnp.where(lane // RW_HEAD == h, x, 0.0) for h in range(ng // RW_HEAD)])

    def split3(logw):
        hi = logw.astype(BF16)
        rem = logw - hi.astype(F32)
        mid = rem.astype(BF16)
        return hi, mid, (rem - mid.astype(F32)).astype(BF16)

    parts = [split3(ch[5]) for ch in chains]
    cums = [d(tri, hi) + d(tri, mid) + d(tri, lo) for hi, mid, lo in parts]

    def decayed(ch, cum):
        r, k2, v, na, b, logw = ch
        cum_c = cum[c - 1:c, :]
        g_inv = jnp.exp(-cum)
        g_tail = jnp.exp(cum_c - cum)
        return dict(at=stack(na * jnp.exp(cum - logw)), rt=stack(r * jnp.exp(cum)),
                    btkt=cat0(stack(b * g_inv), stack(k2 * g_inv)),
                    bhkh=cat0(stack(b * g_tail), stack(k2 * g_tail)), vm=stack(v),
                    gc=jnp.exp(cum_c))

    q = [decayed(ch, cum) for ch, cum in zip(chains, cums)]
    grams = [_dot_nt(cat0(x['at'], x['rt']), x['btkt']) for x in q]
    strict = masks_ref[0]
    incl = masks_ref[1]
    ns = [g[:ng, :ng] * strict for g in grams]
    gaks = [g[:ng, ng:] * strict for g in grams]
    grbs = [g[ng:, :ng] * incl for g in grams]
    grks = [g[ng:, ng:] * incl for g in grams]

    eye = jnp.where(row == col, 1.0, 0.0)
    xs = [eye + n * masks_ref[2] for n in ns]
    for lvl in range(3, masks_ref.shape[0]):
        lvl_mask = masks_ref[lvl]
        ys = [_dot(x, n * lvl_mask) for x, n in zip(xs, ns)]
        xs = [x + _dot(y, x) for x, y in zip(xs, ys)]

    gvs = [_dot(gak, x['vm']) for gak, x in zip(gaks, q)]
    aus = [_dot(x, cat1(y['at'], gv)) for x, y, gv in zip(xs, q, gvs)]
    ts = [_dot(grb, au) for grb, au in zip(grbs, aus)]
    gkvs = [_dot(grk, x['vm']) for grk, x in zip(grks, q)]
    rbs = [x['rt'] + t[:, :ng] for x, t in zip(q, ts)]
    obs = [t[:, ng:] + gkv for t, gkv in zip(ts, gkvs)]
    ms = [jnp.where(row == col, x['gc'], 0.0) + _dot(au[:, :ng].T, x['bhkh'][:ng])
          for x, au in zip(q, aus)]
    zs = [_dot(cat0(au[:, ng:], x['vm']).T, x['bhkh']) for x, au in zip(q, aus)]

    os_ = [_dot_nt(rb, s_old) + ob for rb, s_old, ob in zip(rbs, s_olds, obs)]
    s_news = [_dot(s_old, m) + z for s_old, m, z in zip(s_olds, ms, zs)]
    o_toks = [functools.reduce(jnp.add, [o[h * c:(h + 1) * c] for h in range(ng // RW_HEAD)])
              for o in os_]
    return o_toks, s_news


def _rwkv_chunk_kernel(r_ref, k_ref, v_ref, na_ref, b_ref, lw_ref, masks_ref, tri_ref,
                       o_ref, sout_ref, s_ref, *, bb, n_groups, n_chunks):
    ci = pl.program_id(1)

    @pl.when(ci == 0)
    def _():
        s_ref[...] = jnp.zeros_like(s_ref)

    ids = [(bi, gi) for bi in range(bb) for gi in range(n_groups)]
    sl = lambda gi: slice(gi * RW_NG, (gi + 1) * RW_NG)
    chains = [tuple(ref[bi, :, sl(gi)].astype(F32)
                    for ref in (r_ref, k_ref, v_ref, na_ref, b_ref, lw_ref))
              for bi, gi in ids]
    o_toks, s_news = _rwkv_chunk_chains(chains, masks_ref, tri_ref[...],
                                        [s_ref[bi * n_groups + gi] for bi, gi in ids])
    for (bi, gi), o_tok, s_new in zip(ids, o_toks, s_news):
        o_ref[bi, :, sl(gi)] = o_tok
        s_ref[bi * n_groups + gi] = s_new

    @pl.when(ci == n_chunks - 1)
    def _():
        for bi, gi in ids:
            s = s_ref[bi * n_groups + gi]
            for h in range(RW_GROUP):
                hs = slice(h * RW_HEAD, (h + 1) * RW_HEAD)
                sout_ref[bi, gi * RW_GROUP + h] = s[hs, hs]


def _rwkv_chunked(r, k2, v, na, b, logw, *, bb):
    bsz, s, rd = r.shape
    n_groups = rd // RW_NG
    n_chunks = s // RW_CHUNK
    n_heads = rd // RW_HEAD
    masks = jnp.asarray(_rwkv_masks())
    tri = jnp.asarray(np.tril(np.ones((RW_CHUNK, RW_CHUNK), np.float32))).astype(BF16)
    tok = pl.BlockSpec((bb, RW_CHUNK, rd), lambda bi, ci: (bi, ci, 0))
    return pl.pallas_call(
        functools.partial(_rwkv_chunk_kernel, bb=bb, n_groups=n_groups, n_chunks=n_chunks),
        grid=(bsz // bb, n_chunks),
        in_specs=[tok] * 6 + [pl.BlockSpec(masks.shape, lambda bi, ci: (0, 0, 0)),
                              pl.BlockSpec(tri.shape, lambda bi, ci: (0, 0))],
        out_specs=[tok, pl.BlockSpec((bb, n_heads, RW_HEAD, RW_HEAD), lambda bi, ci: (bi, 0, 0, 0))],
        out_shape=[jax.ShapeDtypeStruct((bsz, s, rd), F32),
                   jax.ShapeDtypeStruct((bsz, n_heads, RW_HEAD, RW_HEAD), F32)],
        scratch_shapes=[pltpu.VMEM((bb * n_groups, RW_NG, RW_NG), F32)],
        compiler_params=_cparams(("parallel", "arbitrary")),
        name="rwkv_chunked",
    )(r, k2, v, na, b, logw, masks, tri)


def _rwkv_step_kernel(s_ref, r_ref, k_ref, na_ref, b_ref, lw_ref, v_ref, sout_ref, o_ref):
    s = s_ref[...]
    sa = jnp.sum(s * na_ref[...], axis=-1, keepdims=True)
    s_new = s * jnp.exp(lw_ref[...]) + sa * b_ref[...] + v_ref[...] * k_ref[...]
    sout_ref[...] = s_new
    o_ref[...] = jnp.sum(s_new * r_ref[...], axis=-1, keepdims=True)


def _rwkv_step(state, r, k2, v, na, b, logw, *, bs):
    db, nh, hd, _ = state.shape
    assert db % bs == 0
    rowv = lambda x: x.reshape(db, nh, 1, hd)
    rspec = pl.BlockSpec((bs, nh, 1, hd), lambda i: (i, 0, 0, 0))
    cspec = pl.BlockSpec((bs, nh, hd, 1), lambda i: (i, 0, 0, 0))
    sspec = pl.BlockSpec((bs, nh, hd, hd), lambda i: (i, 0, 0, 0))
    s_new, o = pl.pallas_call(
        _rwkv_step_kernel,
        grid=(db // bs,),
        in_specs=[sspec, rspec, rspec, rspec, rspec, rspec, cspec],
        out_specs=[sspec, cspec],
        out_shape=[jax.ShapeDtypeStruct(state.shape, F32), jax.ShapeDtypeStruct((db, nh, hd, 1), F32)],
        compiler_params=_cparams(("parallel",)),
        name="rwkv_step",
    )(state, rowv(r), rowv(k2), rowv(na), rowv(b), rowv(logw), v.reshape(db, nh, hd, 1))
    return o.reshape(db, nh * hd), s_new


def _outproj_ln_kernel(oda_ref, orw_ref, bonus_ref, gate_ref, h_ref, w_ref, lg_ref, lb_ref,
                       ones_ref, g_ref, b_ref, o_ref, *, alpha):
    o = orw_ref[...]
    ones_bd = ones_ref[...]
    inv_n = 1.0 / RW_HEAD
    mu = _dot_exact_rhs(o, ones_bd) * inv_n
    d = o - mu
    var = _dot(d * d, ones_bd) * inv_n
    o = d * lax.rsqrt(var + RW_GN_EPS) * lg_ref[...] + lb_ref[...]
    o_rw = (o + bonus_ref[...]) * gate_ref[...]
    da = oda_ref.shape[1]
    mix = (jnp.dot(oda_ref[...], w_ref[0:da, :], preferred_element_type=F32)
           + jnp.dot(o_rw.astype(BF16), w_ref[da:, :], preferred_element_type=F32))
    o_ref[...] = _layernorm(alpha * h_ref[...] + mix, g_ref[...], b_ref[...])


def _outproj_ln(oda, orw, bonus, gate, h, w_out, lnx_g, lnx_b, ones_bd, g, b, *, alpha, tm):
    m, d = h.shape
    da = oda.shape[1]
    rd = orw.shape[1]
    tok = lambda n: pl.BlockSpec((tm, n), lambda i: (i, 0))
    consts = [w_out, lnx_g, lnx_b, ones_bd, g, b]
    return pl.pallas_call(
        functools.partial(_outproj_ln_kernel, alpha=alpha),
        grid=(m // tm,),
        in_specs=[tok(da), tok(rd), tok(rd), tok(rd), tok(d)] + _weight_specs(consts, 1),
        out_specs=tok(d),
        out_shape=jax.ShapeDtypeStruct((m, d), F32),
        compiler_params=_cparams(("parallel",)),
        name="outproj_ln",
    )(oda, orw, bonus, gate, h, *consts)


def _tile(m, pref):
    return pref if m % pref == 0 else m


def kernel(x_prompt, x_sample, cache_k, cache_v, state_wkv, state_shift, page_table, ffa_w_gu, ffa_w_down, ffb_w_gu, ffb_w_down, ln1_g, ln1_b, ln2_g, ln2_b, ln3_g, ln3_b, w_in, w_out, da_lq1, da_lk1, da_lq2, da_lk2, da_subln_g, rw_mu, rw_w0, rw_w2, rw_a0, rw_a2, rw_g2, rw_k_k, rw_k_a, rw_r_k, rw_lnx_g, rw_lnx_b):
    bsz, seq, d_model = x_prompt.shape
    db, dec_seq, _ = x_sample.shape
    assert dec_seq == 1, "the sample group is written for one new token per sequence"
    depth = w_in.shape[0]
    n_pool, page, n_heads, _ = cache_k.shape[1:]
    n_pages = page_table.shape[1]
    rd = rw_w0.shape[1]
    rw_cols = rw_mu.shape[1]
    rw_heads = rd // RW_HEAD
    d_ff = ffa_w_down.shape[1]
    alpha = (2 * depth) ** 0.25
    width = n_heads * V7X_LANES

    cos_p, sin_p = _rope_tables(jnp.arange(seq))
    cos_s, sin_s = _rope_tables(jnp.full((db,), n_pages * page, jnp.int32))
    ones_bd = jnp.asarray(np.kron(np.eye(rw_heads, dtype=np.float32),
                                  np.ones((RW_HEAD, RW_HEAD), np.float32))).astype(BF16)

    m_p = bsz * seq
    tm_p = _tile(m_p, TOKEN_TILE)
    tf = _tile(d_ff, FFN_CHUNK)
    y_p = x_prompt.reshape(m_p, d_model)
    y_s = x_sample.reshape(db, d_model)
    outs = [[] for _ in range(8)]
    row = lambda a: a[None, :]

    for l in range(depth):
        lam_init = 0.8 - 0.6 * math.exp(-0.3 * l)
        wa_gu, wa_dn = ffa_w_gu[l].astype(BF16), ffa_w_down[l].astype(BF16)
        wb_gu, wb_dn = ffb_w_gu[l].astype(BF16), ffb_w_down[l].astype(BF16)
        wi, wo = w_in[l].astype(BF16), w_out[l].astype(BF16)
        lam_vecs = [row(da_lq1[l]), row(da_lk1[l]), row(da_lq2[l]), row(da_lk2[l]), row(da_subln_g[l])]
        rw_weights = _rwkv_weights(dict(rw_mu=rw_mu[l], rw_w0=rw_w0[l], rw_w2=rw_w2[l], rw_a0=rw_a0[l],
                                        rw_a2=rw_a2[l], rw_g2=rw_g2[l], rw_k_k=rw_k_k[l],
                                        rw_k_a=rw_k_a[l], rw_r_k=rw_r_k[l]), rd)
        rw_consts = rw_weights + [ones_bd]
        ln = [row(a[l]) for a in (ln1_g, ln1_b, ln2_g, ln2_b, ln3_g, ln3_b)]
        lnx = [row(rw_lnx_g[l]), row(rw_lnx_b[l])]

        h1 = _ffn_ln(y_p, wa_gu, wa_dn, ln[0], ln[1], alpha=alpha, tm=tm_p, tf=tf)
        shift0 = jnp.zeros((bsz, 1, rw_cols), F32)
        q, k, kb, v, vb, shift_p, r_, k2_, v_, na_, b_, lw_, gate, bonus = _inproj(
            h1, wi, cos_p, sin_p, shift0, rw_consts, n_heads=n_heads, tm=tm_p, seq_len=seq,
            rw_dtype=BF16)
        oda = _attn_prompt(q.reshape(bsz, seq, width), kb.reshape(bsz, seq, width),
                           vb.reshape(bsz, seq, width), *lam_vecs,
                           n_heads=n_heads, tq=_tile(seq, ATTN_BLOCK), rs=ATTN_ROW_CHUNK,
                           lam_init=lam_init)
        seq3 = lambda a: a.reshape(bsz, seq, rd)
        orw, wkv_p = _rwkv_chunked(seq3(r_), seq3(k2_), seq3(v_), seq3(na_), seq3(b_), seq3(lw_),
                                   bb=_tile(bsz, RW_SEQS_PER_STEP))
        h2 = _outproj_ln(oda.reshape(m_p, width), orw.reshape(m_p, rd), bonus, gate, h1, wo, *lnx,
                         ones_bd, ln[2], ln[3], alpha=alpha, tm=tm_p)
        y_p = _ffn_ln(h2, wb_gu, wb_dn, ln[4], ln[5], alpha=alpha, tm=tm_p, tf=tf)
        outs[0].append(k.reshape(bsz, seq, n_heads, 2 * DA_QK))
        outs[1].append(v.reshape(bsz, seq, n_heads, DA_V))
        outs[2].append(wkv_p)
        outs[3].append(shift_p)

        h1 = _ffn_ln(y_s, wa_gu, wa_dn, ln[0], ln[1], alpha=alpha, tm=db, tf=tf)
        q, k, kb, v, vb, shift_s, r_, k2_, v_, na_, b_, lw_, gate, bonus = _inproj(
            h1, wi, cos_s, sin_s, state_shift[l].reshape(db, rw_cols), rw_consts, n_heads=n_heads,
            tm=db, seq_len=None, rw_dtype=F32)
        oda = _attn_decode(q, kb, vb, cache_k.reshape(depth * n_pool, page * n_heads, V7X_LANES),
                           cache_v.reshape(depth * n_pool, page * n_heads, DA_V), page_table, l * n_pool,
                           *lam_vecs, n_heads=n_heads, gp=_tile(n_pages, DECODE_PAGES_PER_STEP),
                           lam_init=lam_init)
        orw, wkv_s = _rwkv_step(state_wkv[l], r_, k2_, v_, na_, b_, lw_,
                                bs=_tile(db, RW_STEP_SEQS))
        h2 = _outproj_ln(oda, orw, bonus, gate, h1, wo, *lnx, ones_bd, ln[2], ln[3],
                         alpha=alpha, tm=db)
        y_s = _ffn_ln(h2, wb_gu, wb_dn, ln[4], ln[5], alpha=alpha, tm=db, tf=tf)
        outs[4].append(k.reshape(db, 1, n_heads, 2 * DA_QK))
        outs[5].append(v.reshape(db, 1, n_heads, DA_V))
        outs[6].append(wkv_s)
        outs[7].append(shift_s.reshape(db, 1, rw_cols))

    stacked = [jnp.stack(o) for o in outs]
    return (y_p.reshape(bsz, seq, d_model), y_s.reshape(db, 1, d_model), *stacked)
```
